```python
import jax, jax.numpy as jnp
from jax import lax
import numpy as np

D_MODEL = 1024
BATCH = 8
SEQ = 2048
DEPTH = 2
DEC_BATCH = 128
DEC_SEQ = 8
PAST_LEN = 8192
PAGE_SIZE = 128

D_MIX = D_MODEL
RET_HEADS = 4
RET_DK = D_MIX // 2 // RET_HEADS
RET_DV = RET_DK
RET_WIDTH = RET_HEADS * RET_DV
RET_CHUNK = 128
MLA_HEADS = 4
MLA_NOPE = 128
MLA_ROPE = 64
MLA_DV = (D_MIX - RET_WIDTH) // MLA_HEADS
MLA_WIDTH = MLA_HEADS * MLA_DV
Q_RANK = 384
KV_RANK = 256
MLA_SCALE = (MLA_NOPE + MLA_ROPE) ** -0.5
ATTN_BLOCK = 128
ROPE_BASE = 10000.0
EPS = 1e-6
GN_EPS = 1e-5
NEG_INF = -1e30
IN_COLS = 2 * RET_HEADS * RET_DK + 2 * RET_WIDTH + Q_RANK + KV_RANK + MLA_ROPE + MLA_WIDTH

kernel_name = "hymba_retention_mla_decode_step"


def _split_points():
    sizes = (RET_HEADS * RET_DK, RET_HEADS * RET_DK, RET_WIDTH, RET_WIDTH,
             Q_RANK, KV_RANK, MLA_ROPE, MLA_WIDTH)
    return tuple(int(v) for v in np.cumsum(sizes)[:-1])


def rmsnorm(x, w):
    xf = x.astype(jnp.float32)
    y = xf * lax.rsqrt(jnp.mean(xf * xf, axis=-1, keepdims=True) + EPS)
    return (y * w.astype(jnp.float32)).astype(x.dtype)


def rope_cos_sin(pos, dim):
    inv = ROPE_BASE ** (-(jnp.arange(0, dim, 2, dtype=jnp.float32) / dim))
    ang = pos.astype(jnp.float32)[:, None] * inv[None, :]
    return jnp.cos(ang), jnp.sin(ang)


def apply_rope(x, cos, sin):
    xf = x.astype(jnp.float32)
    half = xf.shape[-1] // 2
    x1, x2 = xf[..., :half], xf[..., half:]
    return jnp.concatenate([x1 * cos - x2 * sin, x2 * cos + x1 * sin], axis=-1).astype(x.dtype)


def head_groupnorm(o, w):
    mu = jnp.mean(o, axis=-1, keepdims=True)
    d = o - mu
    var = jnp.mean(d * d, axis=-1, keepdims=True)
    return d * lax.rsqrt(var + GN_EPS) * w.astype(jnp.float32).reshape(RET_HEADS, RET_DV)


def retention_scan(q, k, v, s0, chunk):
    b, t, h, _ = q.shape
    dv = v.shape[-1]
    n = t // chunk
    log_g = jnp.log1p(-jnp.exp2(-5.0 - jnp.arange(h, dtype=jnp.float32)))
    i = jnp.arange(chunk, dtype=jnp.float32)
    diff = i[:, None] - i[None, :]
    intra = jnp.where(diff >= 0, jnp.exp(log_g[:, None, None] * jnp.maximum(diff, 0.0)), 0.0)
    q_dec = jnp.exp(log_g[None, :] * (i[:, None] + 1.0))
    k_dec = jnp.exp(log_g[None, :] * (chunk - 1.0 - i[:, None]))
    s_dec = jnp.exp(log_g * chunk)

    def to_chunks(a):
        return a.reshape(b, n, chunk, h, a.shape[-1]).transpose(1, 0, 2, 3, 4)

    def step(s, xs):
        qc, kc, vc = xs
        a = jnp.einsum('bihd,bjhd->bhij', qc, kc) * intra
        o = jnp.einsum('bhij,bjhe->bihe', a, vc)
        o = o + jnp.einsum('bihd,bhde->bihe', qc, s) * q_dec[None, :, :, None]
        s = s * s_dec[None, :, None, None] + jnp.einsum(
            'bjhd,bjhe->bhde', kc * k_dec[None, :, :, None], vc)
        return s, o

    s, o = lax.scan(step, s0, (to_chunks(q), to_chunks(k), to_chunks(v)))
    o = o.transpose(1, 0, 2, 3, 4).reshape(b, t, h, dv)
    return o, s


def mla_keys_values(c, p):
    lead = c.shape[:-1]
    kn = rmsnorm((c @ p['w_kb']).reshape(*lead, MLA_HEADS, MLA_NOPE), p['kn_w'])
    v = (c @ p['w_vb']).reshape(*lead, MLA_HEADS, MLA_DV)
    return kn, v


def mla_softmax(qn, qp, kn, kp, v, mask):
    s = jnp.einsum('bqhd,bkhd->bhqk', qn, kn) + jnp.einsum('bqhd,bkd->bhqk', qp, kp)
    s = jnp.where(mask, s.astype(jnp.float32) * MLA_SCALE, NEG_INF)
    pr = jax.nn.softmax(s, axis=-1).astype(v.dtype)
    return jnp.einsum('bhqk,bkhd->bqhd', pr, v)


def mla_attend_prompt(qn, qp, c, kpe, p):
    b, t = c.shape[:2]
    kn, v = mla_keys_values(c, p)
    nb = t // ATTN_BLOCK

    def blocks(a):
        return a.reshape(b, nb, ATTN_BLOCK, *a.shape[2:]).swapaxes(0, 1)

    k_pos = jnp.arange(t)

    def one(xs):
        qn_b, qp_b, blk = xs
        q_pos = blk * ATTN_BLOCK + jnp.arange(ATTN_BLOCK)
        mask = k_pos[None, :] <= q_pos[:, None]
        return mla_softmax(qn_b, qp_b, kn, kpe, v, mask)

    o = lax.map(one, (blocks(qn), blocks(qp), jnp.arange(nb)))
    return o.swapaxes(0, 1).reshape(b, t, MLA_HEADS, MLA_DV)


def mla_attend_sample(qn, qp, c, kpe, p, cache_ckv, cache_kpe, page_table, layer):
    past = page_table.shape[1] * PAGE_SIZE
    t = c.shape[1]

    def one(xs):
        pt, qn_s, qp_s, c_s, kpe_s = xs
        c_all = jnp.concatenate([cache_ckv[layer, pt].reshape(-1, KV_RANK).astype(c_s.dtype), c_s], axis=0)
        kp_all = jnp.concatenate([cache_kpe[layer, pt].reshape(-1, MLA_ROPE).astype(kpe_s.dtype), kpe_s], axis=0)
        kn, v = mla_keys_values(c_all[None], p)
        q_pos = past + jnp.arange(t)
        mask = jnp.arange(c_all.shape[0])[None, :] <= q_pos[:, None]
        return mla_softmax(qn_s[None], qp_s[None], kn, kp_all[None], v, mask)[0]

    return lax.map(one, (page_table, qn, qp, c, kpe))


def layer_forward(x, pos, p, ret_s0, ret_chunk, attend):
    b, t, _ = x.shape
    h = rmsnorm(x, p['norm_w'])
    z = h @ p['w_in']
    q_r, k_r, v_r, g_r, q_lat, c_raw, kpe_raw, g_a = jnp.split(z, _split_points(), axis=-1)
    cos_r, sin_r = rope_cos_sin(pos, RET_DK)
    q_r = apply_rope(q_r.reshape(b, t, RET_HEADS, RET_DK).astype(jnp.float32), cos_r[:, None, :], sin_r[:, None, :])
    k_r = apply_rope(k_r.reshape(b, t, RET_HEADS, RET_DK).astype(jnp.float32), cos_r[:, None, :], sin_r[:, None, :]) * (RET_DK ** -0.5)
    v_r = v_r.reshape(b, t, RET_HEADS, RET_DV).astype(jnp.float32)
    o_r, s_new = retention_scan(q_r, k_r, v_r, ret_s0.astype(jnp.float32), ret_chunk)
    o_r = head_groupnorm(o_r, p['ret_gn_w']).reshape(b, t, RET_WIDTH).astype(x.dtype) * jax.nn.silu(g_r)
    q = (rmsnorm(q_lat, p['q_a_norm_w']) @ p['w_qb']).reshape(b, t, MLA_HEADS, MLA_NOPE + MLA_ROPE)
    cos_a, sin_a = rope_cos_sin(pos, MLA_ROPE)
    qn = rmsnorm(q[..., :MLA_NOPE], p['qn_w'])
    qp = apply_rope(rmsnorm(q[..., MLA_NOPE:], p['qp_w']), cos_a[:, None, :], sin_a[:, None, :])
    c = rmsnorm(c_raw, p['kv_a_norm_w'])
    kpe = apply_rope(rmsnorm(kpe_raw, p['kp_w']), cos_a, sin_a)
    o_a = attend(qn, qp, c, kpe, p).reshape(b, t, MLA_WIDTH) * jax.nn.silu(g_a)
    y = jnp.concatenate([o_r, o_a], axis=-1) @ p['w_out']
    return x + y, c, kpe, s_new.astype(x.dtype)


def setup_inputs(seed: int = 0) -> dict:
    key = jax.random.key(seed)
    ks = jax.random.split(key, 24)
    f32 = jnp.float32
    n_pages = PAST_LEN // PAGE_SIZE
    n_used = DEC_BATCH * n_pages
    n_pool = n_used + n_used // 4

    def nrm(k, shape, scale):
        return jax.random.normal(k, shape, f32) * scale

    def gain(k, n):
        return 1.0 + 0.02 * jax.random.normal(k, (DEPTH, n), f32)

    page_table = jax.random.permutation(ks[0], n_pool)[:n_used].reshape(DEC_BATCH, n_pages).astype(jnp.int32)
    return {
        "x_prompt": nrm(ks[1], (BATCH, SEQ, D_MODEL), 1.0),
        "x_sample": nrm(ks[2], (DEC_BATCH, DEC_SEQ, D_MODEL), 1.0),
        "cache_ckv": nrm(ks[3], (DEPTH, n_pool, PAGE_SIZE, KV_RANK), 1.0),
        "cache_kpe": nrm(ks[4], (DEPTH, n_pool, PAGE_SIZE, MLA_ROPE), 1.0),
        "state_ret": nrm(ks[5], (DEPTH, DEC_BATCH, RET_HEADS, RET_DK, RET_DV), 0.5),
        "page_table": page_table,
        "norm_w": gain(ks[6], D_MODEL),
        "w_in": nrm(ks[7], (DEPTH, D_MODEL, IN_COLS), D_MODEL ** -0.5),
        "q_a_norm_w": gain(ks[8], Q_RANK),
        "w_qb": nrm(ks[9], (DEPTH, Q_RANK, MLA_HEADS * (MLA_NOPE + MLA_ROPE)), Q_RANK ** -0.5),
        "qn_w": gain(ks[10], MLA_NOPE),
        "qp_w": gain(ks[11], MLA_ROPE),
        "kv_a_norm_w": gain(ks[12], KV_RANK),
        "kp_w": gain(ks[13], MLA_ROPE),
        "w_kb": nrm(ks[14], (DEPTH, KV_RANK, MLA_HEADS * MLA_NOPE), KV_RANK ** -0.5),
        "kn_w": gain(ks[15], MLA_NOPE),
        "w_vb": nrm(ks[16], (DEPTH, KV_RANK, MLA_HEADS * MLA_DV), KV_RANK ** -0.5),
        "ret_gn_w": gain(ks[17], RET_WIDTH),
        "w_out": nrm(ks[18], (DEPTH, D_MIX, D_MODEL), D_MIX ** -0.5),
    }


def reference(x_prompt, x_sample, cache_ckv, cache_kpe, state_ret, page_table,
              norm_w, w_in, q_a_norm_w, w_qb, qn_w, qp_w, kv_a_norm_w, kp_w,
              w_kb, kn_w, w_vb, ret_gn_w, w_out):
    b_p, t_p, _ = x_prompt.shape
    t_s = x_sample.shape[1]
    past = page_table.shape[1] * PAGE_SIZE
    pos_p = jnp.arange(t_p, dtype=jnp.int32)
    pos_s = past + jnp.arange(t_s, dtype=jnp.int32)
    ret_zero = jnp.zeros((b_p, RET_HEADS, RET_DK, RET_DV), jnp.float32)

    hp, hs = x_prompt, x_sample
    ckv_p, kpe_p, ret_p, ckv_s, kpe_s, ret_s = [], [], [], [], [], []
    for l in range(DEPTH):
        p = {"norm_w": norm_w[l], "w_in": w_in[l], "q_a_norm_w": q_a_norm_w[l], "w_qb": w_qb[l],
             "qn_w": qn_w[l], "qp_w": qp_w[l], "kv_a_norm_w": kv_a_norm_w[l], "kp_w": kp_w[l],
             "w_kb": w_kb[l], "kn_w": kn_w[l], "w_vb": w_vb[l], "ret_gn_w": ret_gn_w[l], "w_out": w_out[l]}
        hp, c1, k1, s1 = layer_forward(hp, pos_p, p, ret_zero, RET_CHUNK, mla_attend_prompt)
        attend_s = lambda qn, qp, c, kpe, pp, _l=l: mla_attend_sample(
            qn, qp, c, kpe, pp, cache_ckv, cache_kpe, page_table, _l)
        hs, c2, k2, s2 = layer_forward(hs, pos_s, p, state_ret[l], t_s, attend_s)
        ckv_p.append(c1); kpe_p.append(k1); ret_p.append(s1)
        ckv_s.append(c2); kpe_s.append(k2); ret_s.append(s2)

    return (hp, hs, jnp.stack(ckv_p), jnp.stack(kpe_p), jnp.stack(ret_p),
            jnp.stack(ckv_s), jnp.stack(kpe_s), jnp.stack(ret_s))
```

```python
import functools

import numpy as np
import jax
import jax.numpy as jnp
from jax import lax
from jax.experimental import pallas as pl
from jax.experimental.pallas import tpu as pltpu

F32 = jnp.float32
BF16 = jnp.bfloat16

D_MODEL = 1024
PAGE_SIZE = 128
RET_HEADS = 4
RET_DK = 128
RET_DV = 128
RET_WIDTH = RET_HEADS * RET_DV
RET_CHUNK = 128
MLA_HEADS = 4
MLA_NOPE = 128
MLA_ROPE = 64
MLA_DV = 128
MLA_WIDTH = MLA_HEADS * MLA_DV
Q_RANK = 384
KV_RANK = 256
MLA_SCALE = (MLA_NOPE + MLA_ROPE) ** -0.5
ROPE_BASE = 10000.0
EPS = 1e-6
GN_EPS = 1e-5
NEG_INF = -1e30

LANES = 128
SUBLANES = 8
VMEM_LIMIT = 56 * 1024 * 1024

OFF_QR = 0
OFF_KR = 512
OFF_VR = 1024
OFF_GR = 1536
OFF_GA = 2048
OFF_QL = 2560
OFF_C = OFF_QL + Q_RANK
OFF_KPE = OFF_C + KV_RANK
IN_COLS_PAD = OFF_KPE + LANES
QHEAD = 2 * LANES
QCOLS = MLA_HEADS * QHEAD

NT_DIMS = (((1,), (1,)), ((), ()))
TN_DIMS = (((0,), (0,)), ((), ()))


def _dot(a, b):
    return jnp.dot(a, b, preferred_element_type=F32)


def _dot_nt(a, b):
    return lax.dot_general(a, b, NT_DIMS, preferred_element_type=F32)


def _dot_tn(a, b):
    return lax.dot_general(a, b, TN_DIMS, preferred_element_type=F32)


def _rms_scale(v, n):
    return lax.rsqrt(jnp.sum(v * v, axis=-1, keepdims=True) * (1.0 / n) + EPS)


def _rope64(u, cos, sin, lane):
    swapped = jnp.where(lane < MLA_ROPE // 2,
                        pltpu.roll(u, LANES - MLA_ROPE // 2, 1),
                        pltpu.roll(u, MLA_ROPE // 2, 1))
    return u * cos + swapped * sin


def _front_kernel(emit_kv, x_ref, cosr_ref, sinr_ref, cosa_ref, sina_ref, normw_ref, win_ref,
                  qanw_ref, wqb_ref, qnw_ref, qpw_ref, kvnw_ref, kpw_ref, wkb_ref, knw_ref,
                  wvb_ref, qr_ref, kr_ref, vr_ref, gr_ref, ga_ref, q_ref, c_ref, kpe_ref,
                  *kv_refs):
    x = x_ref[...]
    h = (x * _rms_scale(x, D_MODEL) * normw_ref[...]).astype(BF16)
    cosr, sinr = cosr_ref[...], sinr_ref[...]
    cosa, sina = cosa_ref[...], sina_ref[...]
    lane = lax.broadcasted_iota(jnp.int32, cosa.shape, 1)

    zq = _dot(h, win_ref[:, OFF_QR:OFF_QR + 512])
    zk = _dot(h, win_ref[:, OFF_KR:OFF_KR + 512])
    for hh in range(RET_HEADS):
        sl = slice(hh * RET_DK, (hh + 1) * RET_DK)
        qh, kh = zq[:, sl], zk[:, sl]
        qr_ref[:, sl] = (qh * cosr + pltpu.roll(qh, RET_DK // 2, 1) * sinr).astype(qr_ref.dtype)
        kr_ref[:, sl] = ((kh * cosr + pltpu.roll(kh, RET_DK // 2, 1) * sinr)
                         * (RET_DK ** -0.5)).astype(kr_ref.dtype)
    vr_ref[...] = _dot(h, win_ref[:, OFF_VR:OFF_VR + 512]).astype(vr_ref.dtype)
    zg = _dot(h, win_ref[:, OFF_GR:OFF_GR + 512])
    gr_ref[...] = (zg * jax.nn.sigmoid(zg)).astype(gr_ref.dtype)
    zg = _dot(h, win_ref[:, OFF_GA:OFF_GA + 512])
    ga_ref[...] = (zg * jax.nn.sigmoid(zg)).astype(ga_ref.dtype)

    zql = _dot(h, win_ref[:, OFF_QL:OFF_QL + Q_RANK])
    qa = (zql * _rms_scale(zql, Q_RANK) * qanw_ref[...]).astype(BF16)
    q = _dot(qa, wqb_ref[...])
    for hh in range(MLA_HEADS):
        base = hh * QHEAD
        qn = q[:, base:base + LANES]
        qn = qn * _rms_scale(qn, MLA_NOPE) * qnw_ref[...] * MLA_SCALE
        qp = q[:, base + LANES:base + QHEAD]
        u = qp * _rms_scale(qp, MLA_ROPE) * qpw_ref[...]
        qp = _rope64(u, cosa, sina, lane) * MLA_SCALE
        q_ref[:, base:base + LANES] = qn.astype(q_ref.dtype)
        q_ref[:, base + LANES:base + QHEAD] = qp.astype(q_ref.dtype)

    zck = _dot(h, win_ref[:, OFF_C:OFF_C + KV_RANK + LANES])
    zc = zck[:, :KV_RANK]
    c = zc * _rms_scale(zc, KV_RANK) * kvnw_ref[...]
    c_ref[...] = c
    zp = zck[:, KV_RANK:]
    u = zp * _rms_scale(zp, MLA_ROPE) * kpw_ref[...]
    kp = _rope64(u, cosa, sina, lane)
    kpe_ref[...] = kp[:, :MLA_ROPE]

    if emit_kv:
        kcat_ref, v_ref = kv_refs
        cb = c.astype(BF16)
        kf = _dot(cb, wkb_ref[...])
        for hh in range(MLA_HEADS):
            kn = kf[:, hh * MLA_NOPE:(hh + 1) * MLA_NOPE]
            kn = kn * _rms_scale(kn, MLA_NOPE) * knw_ref[...]
            kcat_ref[:, hh * QHEAD:hh * QHEAD + LANES] = kn.astype(kcat_ref.dtype)
            kcat_ref[:, hh * QHEAD + LANES:(hh + 1) * QHEAD] = kp.astype(kcat_ref.dtype)
        v_ref[...] = _dot(cb, wvb_ref[...]).astype(v_ref.dtype)


def _front(x, tabs, lw, *, tm, emit_kv, act_dtype):
    n = x.shape[0]
    tper = tabs[0].shape[0]
    nper = tper // tm
    row = lambda i: (i, 0)
    const = lambda i: (0, 0)
    tab_spec = pl.BlockSpec((tm, LANES), lambda i: (i % nper, 0))

    def full(a):
        return pl.BlockSpec(a.shape, const)

    weights = [lw["norm_w"], lw["w_in"], lw["q_a_norm_w"], lw["w_qb"], lw["qn_w"], lw["qp_w"],
               lw["kv_a_norm_w"], lw["kp_w"], lw["w_kb"], lw["kn_w"], lw["w_vb"]]
    in_specs = [pl.BlockSpec((tm, D_MODEL), row)] + [tab_spec] * 4 + [full(w) for w in weights]
    out_shape = [jax.ShapeDtypeStruct((n, 512), BF16)] * 3
    out_shape += [jax.ShapeDtypeStruct((n, 512), BF16)]
    out_shape += [jax.ShapeDtypeStruct((n, 512), act_dtype)]
    out_shape += [jax.ShapeDtypeStruct((n, QCOLS), act_dtype)]
    out_shape += [jax.ShapeDtypeStruct((n, KV_RANK), F32)]
    out_shape += [jax.ShapeDtypeStruct((n, MLA_ROPE), F32)]
    if emit_kv:
        out_shape += [jax.ShapeDtypeStruct((n, QCOLS), BF16)]
        out_shape += [jax.ShapeDtypeStruct((n, MLA_WIDTH), BF16)]
    out_specs = [pl.BlockSpec((tm, s.shape[1]), row) for s in out_shape]
    return pl.pallas_call(
        functools.partial(_front_kernel, emit_kv),
        grid=(n // tm,),
        in_specs=in_specs,
        out_specs=out_specs,
        out_shape=out_shape,
        compiler_params=pltpu.CompilerParams(dimension_semantics=("arbitrary",),
                                             vmem_limit_bytes=VMEM_LIMIT),
        name="front_kv" if emit_kv else "front",
    )(x, *tabs, *weights)


def _ret_prompt_kernel(sdec, q_ref, k_ref, v_ref, g_ref, intra_ref, qdec_ref, kdec_ref, gnw_ref,
                       o_ref, sout_ref, s_ref):
    c = pl.program_id(1)

    @pl.when(c == 0)
    def _():
        s_ref[...] = jnp.zeros_like(s_ref)

    for hh in range(RET_HEADS):
        sl = slice(hh * RET_DK, (hh + 1) * RET_DK)
        q, k, v = q_ref[:, sl], k_ref[:, sl], v_ref[:, sl]
        s = s_ref[hh]
        a = _dot_nt(q, k) * intra_ref[hh]
        o = _dot(a.astype(BF16), v) + _dot(q, s.astype(BF16)) * qdec_ref[:, sl]
        kd = (k.astype(F32) * kdec_ref[:, sl]).astype(BF16)
        s_ref[hh] = s * sdec[hh] + _dot_tn(kd, v)
        mu = jnp.mean(o, axis=-1, keepdims=True)
        d = o - mu
        var = jnp.mean(d * d, axis=-1, keepdims=True)
        on = d * lax.rsqrt(var + GN_EPS) * gnw_ref[:, sl]
        o_ref[:, sl] = (on * g_ref[:, sl].astype(F32)).astype(o_ref.dtype)

    @pl.when(c == pl.num_programs(1) - 1)
    def _():
        sout_ref[0] = s_ref[...]


def _ret_prompt(qr, kr, vr, gr, dec, gnw, *, batch, seq):
    nc = seq // RET_CHUNK
    blk = pl.BlockSpec((RET_CHUNK, RET_WIDTH), lambda b, c: (b * nc + c, 0))
    const2 = lambda b, c: (0, 0)
    return pl.pallas_call(
        functools.partial(_ret_prompt_kernel, dec["sdec"]),
        grid=(batch, nc),
        in_specs=[blk, blk, blk, blk,
                  pl.BlockSpec(dec["intra"].shape, lambda b, c: (0, 0, 0)),
                  pl.BlockSpec(dec["qdec"].shape, const2),
                  pl.BlockSpec(dec["kdec"].shape, const2),
                  pl.BlockSpec(gnw.shape, const2)],
        out_specs=[blk, pl.BlockSpec((1, RET_HEADS, RET_DK, RET_DV), lambda b, c: (b, 0, 0, 0))],
        out_shape=[jax.ShapeDtypeStruct(qr.shape, BF16),
                   jax.ShapeDtypeStruct((batch, RET_HEADS, RET_DK, RET_DV), F32)],
        scratch_shapes=[pltpu.VMEM((RET_HEADS, RET_DK, RET_DV), F32)],
        compiler_params=pltpu.CompilerParams(dimension_semantics=("arbitrary", "arbitrary"),
                                             vmem_limit_bytes=VMEM_LIMIT),
        name="ret_prompt",
    )(qr, kr, vr, gr, dec["intra"], dec["qdec"], dec["kdec"], gnw)


def _ret_sample_kernel(sdec, t, q_ref, k_ref, v_ref, g_ref, st_ref, intra_ref, qdec_ref, kdec_ref,
                       gnw_ref, o_ref, sout_ref):
    nseq = st_ref.shape[0]
    for hh in range(RET_HEADS):
        sl = slice(hh * RET_DK, (hh + 1) * RET_DK)
        q, k, v = q_ref[:, sl], k_ref[:, sl], v_ref[:, sl]
        a = _dot_nt(q, k) * intra_ref[hh]
        o_intra = _dot(a.astype(BF16), v)
        q32 = q.astype(F32)
        v32 = v.astype(F32)
        kd32 = (k.astype(F32) * kdec_ref[:, sl]).astype(BF16).astype(F32)
        qdec = qdec_ref[:, sl]
        gnw = gnw_ref[:, sl]
        for s in range(nseq):
            rows = slice(s * t, (s + 1) * t)
            s0 = st_ref[s, hh]
            o = o_intra[rows] + _dot(q32[rows], s0) * qdec[rows]
            sout_ref[s, hh] = s0 * sdec[hh] + _dot_tn(kd32[rows], v32[rows])
            mu = jnp.mean(o, axis=-1, keepdims=True)
            d = o - mu
            var = jnp.mean(d * d, axis=-1, keepdims=True)
            on = d * lax.rsqrt(var + GN_EPS) * gnw
            o_ref[rows, sl] = (on * g_ref[rows, sl].astype(F32)).astype(o_ref.dtype)


def _ret_sample(qr, kr, vr, gr, state, dec, gnw, *, t, nseq):
    n = qr.shape[0]
    nb = n // t
    rows = nseq * t
    blk = pl.BlockSpec((rows, RET_WIDTH), lambda i: (i, 0))
    sblk = pl.BlockSpec((nseq, RET_HEADS, RET_DK, RET_DV), lambda i: (i, 0, 0, 0))
    const2 = lambda i: (0, 0)
    return pl.pallas_call(
        functools.partial(_ret_sample_kernel, dec["sdec"], t),
        grid=(nb // nseq,),
        in_specs=[blk, blk, blk, blk, sblk,
                  pl.BlockSpec(dec["intra"].shape, lambda i: (0, 0, 0)),
                  pl.BlockSpec(dec["qdec"].shape, const2),
                  pl.BlockSpec(dec["kdec"].shape, const2),
                  pl.BlockSpec(gnw.shape, const2)],
        out_specs=[blk, sblk],
        out_shape=[jax.ShapeDtypeStruct(qr.shape, BF16),
                   jax.ShapeDtypeStruct(state.shape, F32)],
        compiler_params=pltpu.CompilerParams(dimension_semantics=("arbitrary",),
                                             vmem_limit_bytes=VMEM_LIMIT),
        name="ret_sample",
    )(qr, kr, vr, gr, state, dec["intra"], dec["qdec"], dec["kdec"], gnw)


def _attn_prompt_kernel(tq, q_ref, k_ref, v_ref, g_ref, o_ref):
    qi = pl.program_id(1)
    row = lax.broadcasted_iota(jnp.int32, (tq, tq), 0)
    col = lax.broadcasted_iota(jnp.int32, (tq, tq), 1)
    causal = col <= row

    for hh in range(MLA_HEADS):
        qh = q_ref[:, hh * QHEAD:(hh + 1) * QHEAD]

        def block(j, carry, masked):
            m, l, acc = carry
            start = pl.multiple_of(j * tq, tq)
            kj = k_ref[pl.ds(start, tq), hh * QHEAD:(hh + 1) * QHEAD]
            vj = v_ref[pl.ds(start, tq), hh * MLA_DV:(hh + 1) * MLA_DV]
            s = _dot_nt(qh, kj)
            if masked:
                s = jnp.where(causal, s, NEG_INF)
            m_new = jnp.maximum(m, jnp.max(s, axis=-1, keepdims=True))
            alpha = jnp.exp(m - m_new)
            p = jnp.exp(s - m_new)
            l = l * alpha + jnp.sum(p, axis=-1, keepdims=True)
            acc = acc * alpha + _dot(p.astype(BF16), vj)
            return m_new, l, acc

        init = (jnp.full((tq, 1), NEG_INF, F32), jnp.zeros((tq, 1), F32),
                jnp.zeros((tq, MLA_DV), F32))
        carry = lax.fori_loop(0, qi, lambda j, cr: block(j, cr, False), init)
        _, l, acc = block(qi, carry, True)
        gate = g_ref[:, hh * MLA_DV:(hh + 1) * MLA_DV].astype(F32)
        o_ref[:, hh * MLA_DV:(hh + 1) * MLA_DV] = (acc / l * gate).astype(o_ref.dtype)


def _attn_prompt(q, kcat, v, ga, *, batch, seq, tq):
    nq = seq // tq
    return pl.pallas_call(
        functools.partial(_attn_prompt_kernel, tq),
        grid=(batch, nq),
        in_specs=[pl.BlockSpec((tq, QCOLS), lambda b, i: (b * nq + i, 0)),
                  pl.BlockSpec((seq, QCOLS), lambda b, i: (b, 0)),
                  pl.BlockSpec((seq, MLA_WIDTH), lambda b, i: (b, 0)),
                  pl.BlockSpec((tq, MLA_WIDTH), lambda b, i: (b * nq + i, 0))],
        out_specs=pl.BlockSpec((tq, MLA_WIDTH), lambda b, i: (b * nq + i, 0)),
        out_shape=jax.ShapeDtypeStruct((q.shape[0], MLA_WIDTH), BF16),
        compiler_params=pltpu.CompilerParams(dimension_semantics=("arbitrary", "arbitrary"),
                                             vmem_limit_bytes=VMEM_LIMIT),
        name="attn_prompt",
    )(q, kcat, v, ga)


def _col_from_row(row_vec, eye):
    return jnp.sum(jnp.where(eye, jnp.broadcast_to(row_vec, eye.shape), 0.0), axis=1,
                   keepdims=True)


def _page_copies(layer, ppb, pt_ref, ckv_hbm, kpe_hbm, cbuf, kpbuf, sems, step, slot):
    copies = []
    for j in range(ppb):
        page = pt_ref[step * ppb + j]
        rows = pl.ds(j * PAGE_SIZE, PAGE_SIZE)
        copies.append(pltpu.make_async_copy(ckv_hbm.at[layer, page], cbuf.at[slot, rows],
                                            sems.at[slot, 0]))
        copies.append(pltpu.make_async_copy(kpe_hbm.at[layer, page], kpbuf.at[slot, rows],
                                            sems.at[slot, 1]))
    return copies


def _attn_sample_kernel(layer, ppb, t, pt_ref, q_ref, cnew_ref, kpnew_ref, g_ref, wkb_ref, wvb_ref,
                        knw_ref, ones_ref, ckv_hbm, kpe_hbm, o_ref,
                        cbuf, kpbuf, sems, m_ref, l_ref, acc_ref, qt_ref, qpt_ref):
    b = pl.program_id(0)
    blk = pl.program_id(1)
    nblk = pl.num_programs(1)
    step = b * nblk + blk
    nstep = pl.num_programs(0) * nblk
    slot = step % 2
    nq = MLA_HEADS * t
    copies = functools.partial(_page_copies, layer, ppb, pt_ref, ckv_hbm, kpe_hbm, cbuf, kpbuf,
                               sems)

    @pl.when(step == 0)
    def _():
        for cp in copies(step, slot):
            cp.start()

    @pl.when(step + 1 < nstep)
    def _():
        for cp in copies(step + 1, 1 - slot):
            cp.start()

    ri = lax.broadcasted_iota(jnp.int32, (LANES, LANES), 0)
    ci = lax.broadcasted_iota(jnp.int32, (LANES, LANES), 1)
    eye = ri == ci

    def update(c_bf, kp_bf, valid):
        kf = _dot(c_bf, wkb_ref[...])
        ssq = _dot((kf * kf).astype(BF16), ones_ref[...])
        sn = _dot_nt(kf.astype(BF16), qt_ref[...])
        sp = _dot_nt(kp_bf, qpt_ref[...])
        s = sn * lax.rsqrt(ssq * (1.0 / MLA_NOPE) + EPS) + sp
        if valid is not None:
            s = jnp.where(valid, s, NEG_INF)
        m_old = m_ref[0:1, :]
        m_new = jnp.maximum(m_old, jnp.max(s, axis=0, keepdims=True))
        alpha = jnp.exp(m_old - m_new)
        p = jnp.exp(s - m_new)
        l_ref[0:1, :] = l_ref[0:1, :] * alpha + jnp.sum(p, axis=0, keepdims=True)
        m_ref[0:1, :] = m_new
        acc_ref[...] = acc_ref[...] * _col_from_row(alpha, eye) + _dot_tn(p.astype(BF16), c_bf)

    @pl.when(blk == 0)
    def _():
        q = q_ref[...]
        qn = jnp.concatenate([q[:, hh * QHEAD:hh * QHEAD + LANES] for hh in range(MLA_HEADS)],
                             axis=1) * jnp.concatenate([knw_ref[...]] * MLA_HEADS, axis=1)
        qn = jnp.concatenate([qn] * MLA_HEADS + [jnp.zeros((LANES - nq, MLA_HEADS * MLA_NOPE), F32)],
                             axis=0)
        rh = lax.broadcasted_iota(jnp.int32, qn.shape, 0) // t
        lh = lax.broadcasted_iota(jnp.int32, qn.shape, 1) // MLA_NOPE
        qt_ref[...] = jnp.where(rh == lh, qn, 0.0).astype(BF16)
        qp = jnp.concatenate([q[:, hh * QHEAD + LANES:(hh + 1) * QHEAD] for hh in range(MLA_HEADS)]
                             + [jnp.zeros((LANES - nq, LANES), F32)], axis=0)
        qpt_ref[...] = qp[:, :MLA_ROPE].astype(BF16)
        m_ref[...] = jnp.full(m_ref.shape, NEG_INF, F32)
        l_ref[...] = jnp.zeros(l_ref.shape, F32)
        acc_ref[...] = jnp.zeros(acc_ref.shape, F32)
        pad = 2 * SUBLANES - t
        c_new = jnp.concatenate([cnew_ref[...], jnp.zeros((pad, KV_RANK), F32)], axis=0)
        kp_new = jnp.concatenate([kpnew_ref[...], jnp.zeros((pad, MLA_ROPE), F32)], axis=0)
        r = lax.broadcasted_iota(jnp.int32, (2 * SUBLANES, LANES), 0)
        j = lax.broadcasted_iota(jnp.int32, (2 * SUBLANES, LANES), 1)
        update(c_new.astype(BF16), kp_new.astype(BF16), (r < t) & (r <= j % t))

    for cp in copies(step, slot):
        cp.wait()
    update(cbuf[slot].astype(BF16), kpbuf[slot].astype(BF16), None)

    @pl.when(blk == nblk - 1)
    def _():
        linv = 1.0 / l_ref[0:1, :]
        lat = (acc_ref[...] * _col_from_row(linv, eye))[:nq].astype(BF16)
        full = _dot(lat, wvb_ref[...])
        o = jnp.concatenate([full[hh * t:(hh + 1) * t, hh * MLA_DV:(hh + 1) * MLA_DV]
                             for hh in range(MLA_HEADS)], axis=1)
        o_ref[...] = (o * g_ref[...]).astype(o_ref.dtype)


def _attn_sample(page_table, q, c_new, kp_new, ga, w_kb, w_vb, kn_w, cache_ckv, cache_kpe, *,
                 layer, t, ppb):
    nseq, npages = page_table.shape
    nblk = npages // ppb
    rows = ppb * PAGE_SIZE
    head_ones = (np.arange(MLA_HEADS * MLA_NOPE)[:, None] // MLA_NOPE
                 == np.arange(LANES)[None, :] // t) & (np.arange(LANES)[None, :] < MLA_HEADS * t)
    head_ones = jnp.asarray(head_ones, BF16)
    seq = lambda b, k, pt: (b, 0)
    const = lambda b, k, pt: (0, 0)
    grid_spec = pltpu.PrefetchScalarGridSpec(
        num_scalar_prefetch=1,
        grid=(nseq, nblk),
        in_specs=[pl.BlockSpec((t, QCOLS), seq),
                  pl.BlockSpec((t, KV_RANK), seq),
                  pl.BlockSpec((t, MLA_ROPE), seq),
                  pl.BlockSpec((t, MLA_WIDTH), seq),
                  pl.BlockSpec(w_kb.shape, const),
                  pl.BlockSpec(w_vb.shape, const),
                  pl.BlockSpec(kn_w.shape, const),
                  pl.BlockSpec(head_ones.shape, const),
                  pl.BlockSpec(memory_space=pl.ANY),
                  pl.BlockSpec(memory_space=pl.ANY)],
        out_specs=pl.BlockSpec((t, MLA_WIDTH), seq),
        scratch_shapes=[pltpu.VMEM((2, rows, KV_RANK), F32),
                        pltpu.VMEM((2, rows, MLA_ROPE), F32),
                        pltpu.SemaphoreType.DMA((2, 2)),
                        pltpu.VMEM((SUBLANES, LANES), F32),
                        pltpu.VMEM((SUBLANES, LANES), F32),
                        pltpu.VMEM((LANES, KV_RANK), F32),
                        pltpu.VMEM((LANES, MLA_HEADS * MLA_NOPE), BF16),
                        pltpu.VMEM((LANES, MLA_ROPE), BF16)])
    return pl.pallas_call(
        functools.partial(_attn_sample_kernel, layer, ppb, t),
        grid_spec=grid_spec,
        out_shape=jax.ShapeDtypeStruct((nseq * t, MLA_WIDTH), F32),
        compiler_params=pltpu.CompilerParams(dimension_semantics=("arbitrary", "arbitrary"),
                                             vmem_limit_bytes=VMEM_LIMIT),
        name="attn_sample",
    )(page_table.reshape(-1), q, c_new, kp_new, ga, w_kb, w_vb, kn_w, head_ones, cache_ckv,
      cache_kpe)


def _outproj_kernel(or_ref, oa_ref, x_ref, w_ref, y_ref):
    y = _dot(or_ref[...].astype(BF16), w_ref[:RET_WIDTH, :])
    y = y + _dot(oa_ref[...].astype(BF16), w_ref[RET_WIDTH:, :])
    y_ref[...] = x_ref[...] + y


def _outproj(o_r, o_a, x, w_out, *, tm):
    n = x.shape[0]
    row = lambda i: (i, 0)
    return pl.pallas_call(
        _outproj_kernel,
        grid=(n // tm,),
        in_specs=[pl.BlockSpec((tm, RET_WIDTH), row), pl.BlockSpec((tm, MLA_WIDTH), row),
                  pl.BlockSpec((tm, D_MODEL), row), pl.BlockSpec(w_out.shape, lambda i: (0, 0))],
        out_specs=pl.BlockSpec((tm, D_MODEL), row),
        out_shape=jax.ShapeDtypeStruct(x.shape, F32),
        compiler_params=pltpu.CompilerParams(dimension_semantics=("arbitrary",),
                                             vmem_limit_bytes=VMEM_LIMIT),
        name="outproj",
    )(o_r, o_a, x, w_out)


def _rope_tables(pos):
    def cs(dim):
        inv = ROPE_BASE ** (-(jnp.arange(0, dim, 2, dtype=F32) / dim))
        ang = pos.astype(F32)[:, None] * inv[None, :]
        return jnp.cos(ang), jnp.sin(ang)

    cr, sr = cs(RET_DK)
    ca, sa = cs(MLA_ROPE)
    zeros = jnp.zeros((pos.shape[0], LANES - MLA_ROPE), F32)
    return (jnp.concatenate([cr, cr], axis=1), jnp.concatenate([-sr, sr], axis=1),
            jnp.concatenate([ca, ca, zeros], axis=1), jnp.concatenate([-sa, sa, zeros], axis=1))


def _decay_tables(chunk, nseq):
    log_g = jnp.log1p(-jnp.exp2(-5.0 - jnp.arange(RET_HEADS, dtype=F32)))
    i = jnp.arange(chunk, dtype=F32)
    diff = i[:, None] - i[None, :]
    intra = jnp.where(diff >= 0, jnp.exp(log_g[:, None, None] * jnp.maximum(diff, 0.0)), 0.0)
    if nseq > 1:
        same = jnp.eye(nseq, dtype=F32)
        intra = jnp.einsum("ab,hij->haibj", same, intra).reshape(RET_HEADS, nseq * chunk,
                                                                 nseq * chunk)
    q_dec = jnp.exp(log_g[None, :] * (i[:, None] + 1.0))
    k_dec = jnp.exp(log_g[None, :] * (chunk - 1.0 - i[:, None]))
    widen = lambda a: jnp.tile(jnp.repeat(a, RET_DK, axis=1), (nseq, 1))
    g = 1.0 - np.exp2(-5.0 - np.arange(RET_HEADS, dtype=np.float64))
    sdec = tuple(float(v) for v in np.exp(np.log(g) * chunk))
    return {"intra": intra, "qdec": widen(q_dec), "kdec": widen(k_dec), "sdec": sdec}


def _prep_weights(norm_w, w_in, q_a_norm_w, w_qb, qn_w, qp_w, kv_a_norm_w, kp_w, w_kb, kn_w, w_vb,
                  ret_gn_w, w_out):
    depth = w_in.shape[0]
    s = np.cumsum((512, 512, 512, 512, Q_RANK, KV_RANK, MLA_ROPE, MLA_WIDTH))
    w_in_r = jnp.concatenate(
        [w_in[:, :, :s[3]], w_in[:, :, s[6]:s[7]], w_in[:, :, s[3]:s[6]],
         jnp.zeros((depth, D_MODEL, LANES - MLA_ROPE), w_in.dtype)], axis=2).astype(BF16)
    wq = w_qb.reshape(depth, Q_RANK, MLA_HEADS, MLA_NOPE + MLA_ROPE)
    wq = jnp.pad(wq, ((0, 0), (0, 0), (0, 0), (0, QHEAD - MLA_NOPE - MLA_ROPE)))
    wq = wq.reshape(depth, Q_RANK, QCOLS).astype(BF16)
    pad_rope = lambda w: jnp.pad(w, ((0, 0), (0, LANES - MLA_ROPE)))
    layers = []
    for l in range(depth):
        layers.append({
            "norm_w": norm_w[l][None], "w_in": w_in_r[l], "q_a_norm_w": q_a_norm_w[l][None],
            "w_qb": wq[l], "qn_w": qn_w[l][None], "qp_w": pad_rope(qp_w)[l][None],
            "kv_a_norm_w": kv_a_norm_w[l][None], "kp_w": pad_rope(kp_w)[l][None],
            "w_kb": w_kb[l].astype(BF16), "kn_w": kn_w[l][None], "w_vb": w_vb[l].astype(BF16),
            "ret_gn_w": ret_gn_w[l][None], "w_out": w_out[l].astype(BF16)})
    return layers


def kernel(x_prompt, x_sample, cache_ckv, cache_kpe, state_ret, page_table, norm_w, w_in, q_a_norm_w,
           w_qb, qn_w, qp_w, kv_a_norm_w, kp_w, w_kb, kn_w, w_vb, ret_gn_w, w_out):
    b_p, t_p, _ = x_prompt.shape
    b_s, t_s, _ = x_sample.shape
    depth = w_in.shape[0]
    past = page_table.shape[1] * PAGE_SIZE
    tm_p = 512
    tm_s = 512
    ret_nseq = 8
    ppb = 8

    layers = _prep_weights(norm_w, w_in, q_a_norm_w, w_qb, qn_w, qp_w, kv_a_norm_w, kp_w, w_kb, kn_w,
                           w_vb, ret_gn_w, w_out)
    tabs_p = _rope_tables(jnp.arange(t_p, dtype=jnp.int32))
    tabs_s = _rope_tables(past + jnp.arange(t_s, dtype=jnp.int32))
    tabs_s = tuple(jnp.tile(a, (tm_s // t_s, 1)) for a in tabs_s)
    dec_p = _decay_tables(RET_CHUNK, 1)
    dec_s = _decay_tables(t_s, ret_nseq)

    hp = x_prompt.reshape(b_p * t_p, D_MODEL)
    hs = x_sample.reshape(b_s * t_s, D_MODEL)
    outs = {k: [] for k in ("ckv_p", "kpe_p", "ret_p", "ckv_s", "kpe_s", "ret_s")}
    for l in range(depth):
        lw = layers[l]
        qr, kr, vr, gr, ga, q, c, kpe, kcat, v = _front(hp, tabs_p, lw, tm=tm_p, emit_kv=True,
                                                        act_dtype=BF16)
        o_r, s_new = _ret_prompt(qr, kr, vr, gr, dec_p, lw["ret_gn_w"], batch=b_p, seq=t_p)
        o_a = _attn_prompt(q, kcat, v, ga, batch=b_p, seq=t_p, tq=256)
        hp = _outproj(o_r, o_a, hp, lw["w_out"], tm=tm_p)
        outs["ckv_p"].append(c.reshape(b_p, t_p, KV_RANK))
        outs["kpe_p"].append(kpe.reshape(b_p, t_p, MLA_ROPE))
        outs["ret_p"].append(s_new)

        qr, kr, vr, gr, ga, q, c, kpe = _front(hs, tabs_s, lw, tm=tm_s, emit_kv=False,
                                               act_dtype=F32)
        o_r, s_new = _ret_sample(qr, kr, vr, gr, state_ret[l], dec_s, lw["ret_gn_w"], t=t_s,
                                 nseq=ret_nseq)
        o_a = _attn_sample(page_table, q, c, kpe, ga, lw["w_kb"], lw["w_vb"], lw["kn_w"], cache_ckv,
                           cache_kpe, layer=l, t=t_s, ppb=ppb)
        hs = _outproj(o_r, o_a, hs, lw["w_out"], tm=tm_s)
        outs["ckv_s"].append(c.reshape(b_s, t_s, KV_RANK))
        outs["kpe_s"].append(kpe.reshape(b_s, t_s, MLA_ROPE))
        outs["ret_s"].append(s_new)

    return (hp.reshape(b_p, t_p, D_MODEL), hs.reshape(b_s, t_s, D_MODEL),
            jnp.stack(outs["ckv_p"]), jnp.stack(outs["kpe_p"]), jnp.stack(outs["ret_p"]),
            jnp.stack(outs["ckv_s"]), jnp.stack(outs["kpe_s"]), jnp.stack(outs["ret_s"]))
```

```python
import functools

import numpy as np
import jax
import jax.numpy as jnp
from jax import lax
from jax.experimental import pallas as pl
from jax.experimental.pallas import tpu as pltpu

F32 = jnp.float32
BF16 = jnp.bfloat16

D_MODEL = 1024
PAGE_SIZE = 128
RET_HEADS = 4
RET_DK = 128
RET_DV = 128
RET_WIDTH = RET_HEADS * RET_DV
RET_CHUNK = 128
MLA_HEADS = 4
MLA_NOPE = 128
MLA_ROPE = 64
MLA_DV = 128
MLA_WIDTH = MLA_HEADS * MLA_DV
Q_RANK = 384
KV_RANK = 256
MLA_SCALE = (MLA_NOPE + MLA_ROPE) ** -0.5
ROPE_BASE = 10000.0
EPS = 1e-6
GN_EPS = 1e-5
NEG_INF = -1e30

LANES = 128
SUBLANES = 8
VMEM_LIMIT = 56 * 1024 * 1024

OFF_QR = 0
OFF_KR = 512
OFF_VR = 1024
OFF_GR = 1536
OFF_GA = 2048
OFF_QL = 2560
OFF_C = OFF_QL + Q_RANK
OFF_KPE = OFF_C + KV_RANK
IN_COLS_PAD = OFF_KPE + LANES
QHEAD = 2 * LANES
QCOLS = MLA_HEADS * QHEAD

NT_DIMS = (((1,), (1,)), ((), ()))
TN_DIMS = (((0,), (0,)), ((), ()))


def _dot(a, b):
    return jnp.dot(a, b, preferred_element_type=F32)


def _dot_nt(a, b):
    return lax.dot_general(a, b, NT_DIMS, preferred_element_type=F32)


def _dot_tn(a, b):
    return lax.dot_general(a, b, TN_DIMS, preferred_element_type=F32)


def _rms_scale(v, n):
    return lax.rsqrt(jnp.sum(v * v, axis=-1, keepdims=True) * (1.0 / n) + EPS)


def _rope64(u, cos, sin, lane):
    swapped = jnp.where(lane < MLA_ROPE // 2,
                        pltpu.roll(u, LANES - MLA_ROPE // 2, 1),
                        pltpu.roll(u, MLA_ROPE // 2, 1))
    return u * cos + swapped * sin


def _front_kernel(emit_kv, x_ref, cosr_ref, sinr_ref, cosa_ref, sina_ref, normw_ref, win_ref,
                  qanw_ref, wqb_ref, qnw_ref, qpw_ref, kvnw_ref, kpw_ref, wkb_ref, knw_ref,
                  wvb_ref, qr_ref, kr_ref, vr_ref, gr_ref, ga_ref, q_ref, c_ref, kpe_ref,
                  *kv_refs):
    x = x_ref[...]
    h = (x * _rms_scale(x, D_MODEL) * normw_ref[...]).astype(BF16)
    cosr, sinr = cosr_ref[...], sinr_ref[...]
    cosa, sina = cosa_ref[...], sina_ref[...]
    lane = lax.broadcasted_iota(jnp.int32, cosa.shape, 1)

    zq = _dot(h, win_ref[:, OFF_QR:OFF_QR + 512])
    zk = _dot(h, win_ref[:, OFF_KR:OFF_KR + 512])
    for hh in range(RET_HEADS):
        sl = slice(hh * RET_DK, (hh + 1) * RET_DK)
        qh, kh = zq[:, sl], zk[:, sl]
        qr_ref[:, sl] = (qh * cosr + pltpu.roll(qh, RET_DK // 2, 1) * sinr).astype(qr_ref.dtype)
        kr_ref[:, sl] = ((kh * cosr + pltpu.roll(kh, RET_DK // 2, 1) * sinr)
                         * (RET_DK ** -0.5)).astype(kr_ref.dtype)
    vr_ref[...] = _dot(h, win_ref[:, OFF_VR:OFF_VR + 512]).astype(vr_ref.dtype)
    zg = _dot(h, win_ref[:, OFF_GR:OFF_GR + 512])
    gr_ref[...] = (zg * jax.nn.sigmoid(zg)).astype(gr_ref.dtype)
    zg = _dot(h, win_ref[:, OFF_GA:OFF_GA + 512])
    ga_ref[...] = (zg * jax.nn.sigmoid(zg)).astype(ga_ref.dtype)

    zql = _dot(h, win_ref[:, OFF_QL:OFF_QL + Q_RANK])
    qa = (zql * _rms_scale(zql, Q_RANK) * qanw_ref[...]).astype(BF16)
    q = _dot(qa, wqb_ref[...])
    for hh in range(MLA_HEADS):
        base = hh * QHEAD
        qn = q[:, base:base + LANES]
        qn = qn * _rms_scale(qn, MLA_NOPE) * qnw_ref[...] * MLA_SCALE
        qp = q[:, base + LANES:base + QHEAD]
        u = qp * _rms_scale(qp, MLA_ROPE) * qpw_ref[...]
        qp = _rope64(u, cosa, sina, lane) * MLA_SCALE
        q_ref[:, base:base + LANES] = qn.astype(q_ref.dtype)
        q_ref[:, base + LANES:base + QHEAD] = qp.astype(q_ref.dtype)

    zck = _dot(h, win_ref[:, OFF_C:OFF_C + KV_RANK + LANES])
    zc = zck[:, :KV_RANK]
    c = zc * _rms_scale(zc, KV_RANK) * kvnw_ref[...]
    c_ref[...] = c
    zp = zck[:, KV_RANK:]
    u = zp * _rms_scale(zp, MLA_ROPE) * kpw_ref[...]
    kp = _rope64(u, cosa, sina, lane)
    kpe_ref[...] = kp[:, :MLA_ROPE]

    if emit_kv:
        kcat_ref, v_ref = kv_refs
        cb = c.astype(BF16)
        kf = _dot(cb, wkb_ref[...])
        for hh in range(MLA_HEADS):
            kn = kf[:, hh * MLA_NOPE:(hh + 1) * MLA_NOPE]
            kn = kn * _rms_scale(kn, MLA_NOPE) * knw_ref[...]
            kcat_ref[:, hh * QHEAD:hh * QHEAD + LANES] = kn.astype(kcat_ref.dtype)
            kcat_ref[:, hh * QHEAD + LANES:(hh + 1) * QHEAD] = kp.astype(kcat_ref.dtype)
        v_ref[...] = _dot(cb, wvb_ref[...]).astype(v_ref.dtype)
    else:
        (kp128_ref,) = kv_refs
        kp128_ref[...] = kp


def _front(x, tabs, lw, *, tm, emit_kv, act_dtype):
    n = x.shape[0]
    tper = tabs[0].shape[0]
    nper = tper // tm
    row = lambda i: (i, 0)
    const = lambda i: (0, 0)
    tab_spec = pl.BlockSpec((tm, LANES), lambda i: (i % nper, 0))

    def full(a):
        return pl.BlockSpec(a.shape, const)

    weights = [lw["norm_w"], lw["w_in"], lw["q_a_norm_w"], lw["w_qb"], lw["qn_w"], lw["qp_w"],
               lw["kv_a_norm_w"], lw["kp_w"], lw["w_kb"], lw["kn_w"], lw["w_vb"]]
    in_specs = [pl.BlockSpec((tm, D_MODEL), row)] + [tab_spec] * 4 + [full(w) for w in weights]
    out_shape = [jax.ShapeDtypeStruct((n, 512), BF16)] * 3
    out_shape += [jax.ShapeDtypeStruct((n, 512), BF16)]
    out_shape += [jax.ShapeDtypeStruct((n, 512), act_dtype)]
    out_shape += [jax.ShapeDtypeStruct((n, QCOLS), act_dtype)]
    out_shape += [jax.ShapeDtypeStruct((n, KV_RANK), F32)]
    out_shape += [jax.ShapeDtypeStruct((n, MLA_ROPE), F32)]
    if emit_kv:
        out_shape += [jax.ShapeDtypeStruct((n, QCOLS), BF16)]
        out_shape += [jax.ShapeDtypeStruct((n, MLA_WIDTH), BF16)]
    else:
        out_shape += [jax.ShapeDtypeStruct((n, LANES), F32)]
    out_specs = [pl.BlockSpec((tm, s.shape[1]), row) for s in out_shape]
    return pl.pallas_call(
        functools.partial(_front_kernel, emit_kv),
        grid=(n // tm,),
        in_specs=in_specs,
        out_specs=out_specs,
        out_shape=out_shape,
        compiler_params=pltpu.CompilerParams(dimension_semantics=("arbitrary",),
                                             vmem_limit_bytes=VMEM_LIMIT),
        name="front_kv" if emit_kv else "front",
    )(x, *tabs, *weights)


def _ret_prompt_kernel(sdec, q_ref, k_ref, v_ref, g_ref, intra_ref, qdec_ref, kdec_ref, gnw_ref,
                       o_ref, sout_ref, s_ref):
    c = pl.program_id(1)

    @pl.when(c == 0)
    def _():
        s_ref[...] = jnp.zeros_like(s_ref)

    for b in range(q_ref.shape[0]):
        for hh in range(RET_HEADS):
            sl = slice(hh * RET_DK, (hh + 1) * RET_DK)
            q, k, v = q_ref[b, :, sl], k_ref[b, :, sl], v_ref[b, :, sl]
            s = s_ref[b, hh]
            a = _dot_nt(q, k) * intra_ref[hh]
            o = _dot(a.astype(BF16), v) + _dot(q, s.astype(BF16)) * qdec_ref[:, sl]
            kd = (k.astype(F32) * kdec_ref[:, sl]).astype(BF16)
            s_ref[b, hh] = s * sdec[hh] + _dot_tn(kd, v)
            mu = jnp.mean(o, axis=-1, keepdims=True)
            d = o - mu
            var = jnp.mean(d * d, axis=-1, keepdims=True)
            on = d * lax.rsqrt(var + GN_EPS) * gnw_ref[:, sl]
            o_ref[b, :, sl] = (on * g_ref[b, :, sl].astype(F32)).astype(o_ref.dtype)

    @pl.when(c == pl.num_programs(1) - 1)
    def _():
        sout_ref[...] = s_ref[...]


def _ret_prompt(qr, kr, vr, gr, dec, gnw, *, batch, seq, bb):
    nc = seq // RET_CHUNK
    as3d = lambda a: a.reshape(batch, seq, RET_WIDTH)
    blk = pl.BlockSpec((bb, RET_CHUNK, RET_WIDTH), lambda b, c: (b, c, 0))
    const2 = lambda b, c: (0, 0)
    o, s = pl.pallas_call(
        functools.partial(_ret_prompt_kernel, dec["sdec"]),
        grid=(batch // bb, nc),
        in_specs=[blk, blk, blk, blk,
                  pl.BlockSpec(dec["intra"].shape, lambda b, c: (0, 0, 0)),
                  pl.BlockSpec(dec["qdec"].shape, const2),
                  pl.BlockSpec(dec["kdec"].shape, const2),
                  pl.BlockSpec(gnw.shape, const2)],
        out_specs=[blk, pl.BlockSpec((bb, RET_HEADS, RET_DK, RET_DV), lambda b, c: (b, 0, 0, 0))],
        out_shape=[jax.ShapeDtypeStruct((batch, seq, RET_WIDTH), BF16),
                   jax.ShapeDtypeStruct((batch, RET_HEADS, RET_DK, RET_DV), F32)],
        scratch_shapes=[pltpu.VMEM((bb, RET_HEADS, RET_DK, RET_DV), F32)],
        compiler_params=pltpu.CompilerParams(dimension_semantics=("arbitrary", "arbitrary"),
                                             vmem_limit_bytes=VMEM_LIMIT),
        name="ret_prompt",
    )(as3d(qr), as3d(kr), as3d(vr), as3d(gr), dec["intra"], dec["qdec"], dec["kdec"], gnw)
    return o.reshape(batch * seq, RET_WIDTH), s


def _ret_sample_kernel(sdec, t, q_ref, k_ref, v_ref, g_ref, st_ref, intra_ref, qdec_ref, kdec_ref,
                       gnw_ref, o_ref, sout_ref):
    nseq = st_ref.shape[0]
    for hh in range(RET_HEADS):
        sl = slice(hh * RET_DK, (hh + 1) * RET_DK)
        q, k, v = q_ref[:, sl], k_ref[:, sl], v_ref[:, sl]
        a = _dot_nt(q, k) * intra_ref[hh]
        o_intra = _dot(a.astype(BF16), v)
        q32 = q.astype(F32)
        v32 = v.astype(F32)
        kd32 = (k.astype(F32) * kdec_ref[:, sl]).astype(BF16).astype(F32)
        qdec = qdec_ref[:, sl]
        gnw = gnw_ref[:, sl]
        for s in range(nseq):
            rows = slice(s * t, (s + 1) * t)
            s0 = st_ref[s, hh]
            o = o_intra[rows] + _dot(q32[rows], s0) * qdec[rows]
            sout_ref[s, hh] = s0 * sdec[hh] + _dot_tn(kd32[rows], v32[rows])
            mu = jnp.mean(o, axis=-1, keepdims=True)
            d = o - mu
            var = jnp.mean(d * d, axis=-1, keepdims=True)
            on = d * lax.rsqrt(var + GN_EPS) * gnw
            o_ref[rows, sl] = (on * g_ref[rows, sl].astype(F32)).astype(o_ref.dtype)


def _ret_sample(qr, kr, vr, gr, state, dec, gnw, *, layer, t, nseq):
    n = qr.shape[0]
    nb = n // t
    rows = nseq * t
    blk = pl.BlockSpec((rows, RET_WIDTH), lambda i: (i, 0))
    sblk = pl.BlockSpec((nseq, RET_HEADS, RET_DK, RET_DV), lambda i: (i, 0, 0, 0))
    sblk_in = pl.BlockSpec((None, nseq, RET_HEADS, RET_DK, RET_DV), lambda i: (layer, i, 0, 0, 0))
    const2 = lambda i: (0, 0)
    return pl.pallas_call(
        functools.partial(_ret_sample_kernel, dec["sdec"], t),
        grid=(nb // nseq,),
        in_specs=[blk, blk, blk, blk, sblk_in,
                  pl.BlockSpec(dec["intra"].shape, lambda i: (0, 0, 0)),
                  pl.BlockSpec(dec["qdec"].shape, const2),
                  pl.BlockSpec(dec["kdec"].shape, const2),
                  pl.BlockSpec(gnw.shape, const2)],
        out_specs=[blk, sblk],
        out_shape=[jax.ShapeDtypeStruct(qr.shape, BF16),
                   jax.ShapeDtypeStruct(state.shape[1:], F32)],
        compiler_params=pltpu.CompilerParams(dimension_semantics=("arbitrary",),
                                             vmem_limit_bytes=VMEM_LIMIT),
        name="ret_sample",
    )(qr, kr, vr, gr, state, dec["intra"], dec["qdec"], dec["kdec"], gnw)


def _attn_prompt_kernel(tq, tk, q_ref, k_ref, v_ref, g_ref, o_ref, m_ref, acc_ref):
    qi = pl.program_id(1)
    m_ref[...] = jnp.full(m_ref.shape, NEG_INF, F32)
    acc_ref[...] = jnp.zeros(acc_ref.shape, F32)
    row = qi * tq + lax.broadcasted_iota(jnp.int32, (tq, LANES), 0)
    col = lax.broadcasted_iota(jnp.int32, (tq, LANES), 1)
    ones = jnp.ones((tk, LANES), BF16)
    nslab = tk // LANES

    def block(j, masked):
        start = pl.multiple_of(j * tk, tk)
        for hh in range(MLA_HEADS):
            qh = q_ref[:, hh * QHEAD:(hh + 1) * QHEAD]
            kj = k_ref[pl.ds(start, tk), hh * QHEAD:(hh + 1) * QHEAD]
            vj = v_ref[pl.ds(start, tk), hh * MLA_DV:(hh + 1) * MLA_DV]
            s = _dot_nt(qh, kj)
            slabs = [s[:, i * LANES:(i + 1) * LANES] for i in range(nslab)]
            if masked:
                slabs = [jnp.where(start + i * LANES + col <= row, sl, NEG_INF)
                         for i, sl in enumerate(slabs)]
            m_old = m_ref[hh]
            m_new = jnp.maximum(m_old, jnp.max(functools.reduce(jnp.maximum, slabs), axis=-1,
                                               keepdims=True))
            alpha = jnp.exp(m_old - m_new)
            p = jnp.concatenate([jnp.exp(sl - m_new) for sl in slabs], axis=1).astype(BF16)
            pv = _dot(p, jnp.concatenate([vj, ones], axis=1))
            acc_ref[hh] = acc_ref[hh] * jnp.concatenate([alpha, alpha], axis=1) + pv
            m_ref[hh] = m_new

    nfull = qi * (tq // tk)

    def body(j, carry):
        block(j, False)
        return carry

    lax.fori_loop(0, nfull, body, 0)
    for d in range(tq // tk):
        block(nfull + d, True)
    for hh in range(MLA_HEADS):
        gate = g_ref[:, hh * MLA_DV:(hh + 1) * MLA_DV].astype(F32)
        acc = acc_ref[hh]
        o_ref[:, hh * MLA_DV:(hh + 1) * MLA_DV] = (acc[:, :MLA_DV] / acc[:, MLA_DV:]
                                                   * gate).astype(o_ref.dtype)


def _attn_prompt(q, kcat, v, ga, *, batch, seq, tq, tk):
    nq = seq // tq
    return pl.pallas_call(
        functools.partial(_attn_prompt_kernel, tq, tk),
        grid=(batch, nq),
        in_specs=[pl.BlockSpec((tq, QCOLS), lambda b, i: (b * nq + i, 0)),
                  pl.BlockSpec((seq, QCOLS), lambda b, i: (b, 0)),
                  pl.BlockSpec((seq, MLA_WIDTH), lambda b, i: (b, 0)),
                  pl.BlockSpec((tq, MLA_WIDTH), lambda b, i: (b * nq + i, 0))],
        out_specs=pl.BlockSpec((tq, MLA_WIDTH), lambda b, i: (b * nq + i, 0)),
        out_shape=jax.ShapeDtypeStruct((q.shape[0], MLA_WIDTH), BF16),
        scratch_shapes=[pltpu.VMEM((MLA_HEADS, tq, LANES), F32),
                        pltpu.VMEM((MLA_HEADS, tq, 2 * MLA_DV), F32)],
        compiler_params=pltpu.CompilerParams(dimension_semantics=("arbitrary", "arbitrary"),
                                             vmem_limit_bytes=VMEM_LIMIT),
        name="attn_prompt",
    )(q, kcat, v, ga)


def _col_from_row(row_vec, eye):
    return jnp.sum(jnp.where(eye, jnp.broadcast_to(row_vec, eye.shape), 0.0), axis=1,
                   keepdims=True)


def _page_copies(layer, ppb, pt_ref, ckv_hbm, kpet_hbm, cbuf, kpbuf, sems, step, slot):
    copies = []
    for j in range(ppb):
        page = pt_ref[step * ppb + j]
        rows = pl.ds(j * PAGE_SIZE, PAGE_SIZE)
        copies.append(pltpu.make_async_copy(ckv_hbm.at[layer, page], cbuf.at[slot, rows],
                                            sems.at[slot, 0]))
        copies.append(pltpu.make_async_copy(kpet_hbm.at[layer, page],
                                            kpbuf.at[slot, pl.ds(0, MLA_ROPE), rows],
                                            sems.at[slot, 1]))
    return copies


def _attn_sample_kernel(layer, ppb, t, pt_ref, q_ref, cnew_ref, kpnew_ref, g_ref, wkbp_ref, wkb_ref,
                        wvb_ref, knw_ref, ones_ref, ckv_hbm, kpet_hbm, o_ref,
                        cbuf, kpbuf, sems, cb_ref, m_ref, l_ref, acc_ref, qtil_ref, b2t_ref):
    b = pl.program_id(0)
    blk = pl.program_id(1)
    nblk = pl.num_programs(1)
    step = b * nblk + blk
    nstep = pl.num_programs(0) * nblk
    slot = step % 2
    group = MLA_HEADS * t
    nsub = LANES // group
    rows = cbuf.shape[1]
    sub = rows // nsub
    copies = functools.partial(_page_copies, layer, ppb, pt_ref, ckv_hbm, kpet_hbm, cbuf, kpbuf,
                               sems)

    @pl.when(step == 0)
    def _():
        kpbuf[:, MLA_ROPE:, :] = jnp.zeros((2, LANES - MLA_ROPE, rows), F32)
        for g in range(nsub):
            b2t_ref[g, :LANES, :] = ones_ref[g]
        for cp in copies(step, slot):
            cp.start()

    ri = lax.broadcasted_iota(jnp.int32, (LANES, LANES), 0)
    ci = lax.broadcasted_iota(jnp.int32, (LANES, LANES), 1)
    eye = ri == ci

    def scores(c_bf, kp_pad, g):
        kf = _dot(c_bf, wkbp_ref[...])
        sq = kf * kf
        fold = (sq[:, 0:LANES] + sq[:, LANES:2 * LANES]) + (sq[:, 2 * LANES:3 * LANES]
                                                            + sq[:, 3 * LANES:4 * LANES])
        lhs2 = jnp.concatenate([fold.astype(BF16), kp_pad.astype(BF16)], axis=1)
        return _dot(c_bf, qtil_ref[g]), _dot_nt(lhs2, b2t_ref[g])

    def softmax_step(sn, r2, valid):
        s = sn * lax.rsqrt(r2[:, :LANES] * (1.0 / MLA_NOPE) + EPS) + r2[:, LANES:]
        if valid is not None:
            s = jnp.where(valid, s, NEG_INF)
        m_old = m_ref[0:1, :]
        m_new = jnp.maximum(m_old, jnp.max(s, axis=0, keepdims=True))
        alpha = jnp.exp(m_old - m_new)
        p = jnp.exp(s - m_new)
        if valid is not None:
            p = jnp.where(valid, p, 0.0)
        l_ref[0:1, :] = l_ref[0:1, :] * alpha + jnp.sum(p, axis=0, keepdims=True)
        m_ref[0:1, :] = m_new
        return p, _col_from_row(alpha, eye)

    @pl.when(blk == 0)
    def _():
        q = q_ref[...]
        qn = jnp.concatenate([q[:, hh * QHEAD:hh * QHEAD + LANES] for hh in range(MLA_HEADS)],
                             axis=1) * jnp.concatenate([knw_ref[...]] * MLA_HEADS, axis=1)
        qn = jnp.concatenate([qn] * (LANES // t), axis=0)
        rh = (lax.broadcasted_iota(jnp.int32, qn.shape, 0) % group) // t
        lh = lax.broadcasted_iota(jnp.int32, qn.shape, 1) // MLA_NOPE
        qt = jnp.where(rh == lh, qn, 0.0).astype(BF16)
        qtil = _dot_nt(wkb_ref[...], qt)
        lane_g = lax.broadcasted_iota(jnp.int32, qtil.shape, 1) // group
        qp = jnp.concatenate([q[:, hh * QHEAD + LANES:(hh + 1) * QHEAD] for hh in range(MLA_HEADS)],
                             axis=0)
        qp = jnp.concatenate([qp] * nsub, axis=0)
        row_g = lax.broadcasted_iota(jnp.int32, qp.shape, 0) // group
        for g in range(nsub):
            qtil_ref[g] = jnp.where(lane_g == g, qtil, 0.0).astype(BF16)
            b2t_ref[g, LANES:, :] = jnp.concatenate(
                [jnp.zeros(qp.shape, F32), jnp.where(row_g == g, qp, 0.0)], axis=1).astype(BF16)
        m_ref[...] = jnp.full(m_ref.shape, NEG_INF, F32)
        l_ref[...] = jnp.zeros(l_ref.shape, F32)
        pad = 2 * SUBLANES - t
        c_new = jnp.concatenate([cnew_ref[...], jnp.zeros((pad, KV_RANK), F32)], axis=0).astype(BF16)
        kp_new = jnp.concatenate([kpnew_ref[...], jnp.zeros((pad, LANES), F32)], axis=0)
        r = lax.broadcasted_iota(jnp.int32, (2 * SUBLANES, LANES), 0)
        j = lax.broadcasted_iota(jnp.int32, (2 * SUBLANES, LANES), 1)
        sn, r2 = scores(c_new, kp_new, 0)
        p, _ = softmax_step(sn, r2, (r < t) & (r <= j % t) & (j < group))
        acc_ref[...] = _dot_tn(p.astype(BF16), c_new)

    for cp in copies(step, slot):
        cp.wait()
    nxt = jnp.minimum(step + 1, nstep - 1)
    for cp in copies(nxt, 1 - slot):
        cp.start()
    sn = r2 = None
    for g in range(nsub):
        rs = pl.ds(g * sub, sub)
        c_g = cbuf[slot, rs, :].astype(BF16)
        cb_ref[rs, :] = c_g
        sn_g, r2_g = scores(c_g, kpbuf[slot, :, rs].T, g)
        sn = sn_g if sn is None else sn + sn_g
        r2 = r2_g if r2 is None else r2 + r2_g
    p, alpha_col = softmax_step(sn, r2, None)
    lane_g = lax.broadcasted_iota(jnp.int32, p.shape, 1) // group
    p_rows = jnp.concatenate([jnp.where(lane_g == g, p, 0.0) for g in range(nsub)], axis=0)
    acc_ref[...] = acc_ref[...] * alpha_col + _dot_tn(p_rows.astype(BF16), cb_ref[...])

    @pl.when(blk == nblk - 1)
    def _():
        m_col = _col_from_row(m_ref[0:1, :], eye)
        l_col = _col_from_row(l_ref[0:1, :], eye)
        parts = [slice(g * group, (g + 1) * group) for g in range(nsub)]
        m_fin = functools.reduce(jnp.maximum, [m_col[sl] for sl in parts])
        w = [jnp.exp(m_col[sl] - m_fin) for sl in parts]
        l_fin = sum(w[g] * l_col[parts[g]] for g in range(nsub))
        lat = sum(w[g] * acc_ref[parts[g], :] for g in range(nsub)) / l_fin
        full = _dot(lat.astype(BF16), wvb_ref[...])
        o = jnp.concatenate([full[hh * t:(hh + 1) * t, hh * MLA_DV:(hh + 1) * MLA_DV]
                             for hh in range(MLA_HEADS)], axis=1)
        o_ref[...] = (o * g_ref[...]).astype(o_ref.dtype)

    @pl.when(step == nstep - 1)
    def _():
        for cp in copies(nxt, 1 - slot):
            cp.wait()


def _attn_sample(page_table, q, c_new, kp_new, ga, w_kb, w_vb, kn_w, cache_ckv, cache_kpet, *,
                 layer, t, ppb):
    nseq, npages = page_table.shape
    nblk = npages // ppb
    rows = ppb * PAGE_SIZE
    group = MLA_HEADS * t
    nsub = LANES // group
    fold = MLA_NOPE // nsub
    w_kbp = w_kb.reshape(KV_RANK, MLA_HEADS, nsub, fold).transpose(0, 2, 1, 3).reshape(w_kb.shape)
    j = np.arange(LANES)
    d = np.arange(2 * LANES)
    ones_top = ((d[None, None, :] < LANES)
                & (d[None, None, :] // fold == ((j % group) // t)[None, :, None])
                & (j[None, :, None] // group == np.arange(nsub)[:, None, None]))
    ones_top = jnp.asarray(ones_top, BF16)
    seq = lambda b, k, pt: (b, 0)
    const = lambda b, k, pt: (0, 0)
    grid_spec = pltpu.PrefetchScalarGridSpec(
        num_scalar_prefetch=1,
        grid=(nseq, nblk),
        in_specs=[pl.BlockSpec((t, QCOLS), seq),
                  pl.BlockSpec((t, KV_RANK), seq),
                  pl.BlockSpec((t, LANES), seq),
                  pl.BlockSpec((t, MLA_WIDTH), seq),
                  pl.BlockSpec(w_kbp.shape, const),
                  pl.BlockSpec(w_kb.shape, const),
                  pl.BlockSpec(w_vb.shape, const),
                  pl.BlockSpec(kn_w.shape, const),
                  pl.BlockSpec(ones_top.shape, lambda b, k, pt: (0, 0, 0)),
                  pl.BlockSpec(memory_space=pl.ANY),
                  pl.BlockSpec(memory_space=pl.ANY)],
        out_specs=pl.BlockSpec((t, MLA_WIDTH), seq),
        scratch_shapes=[pltpu.VMEM((2, rows, KV_RANK), F32),
                        pltpu.VMEM((2, LANES, rows), F32),
                        pltpu.SemaphoreType.DMA((2, 2)),
                        pltpu.VMEM((rows, KV_RANK), BF16),
                        pltpu.VMEM((SUBLANES, LANES), F32),
                        pltpu.VMEM((SUBLANES, LANES), F32),
                        pltpu.VMEM((LANES, KV_RANK), F32),
                        pltpu.VMEM((nsub, KV_RANK, LANES), BF16),
                        pltpu.VMEM((nsub, 2 * LANES, 2 * LANES), BF16)])
    return pl.pallas_call(
        functools.partial(_attn_sample_kernel, layer, ppb, t),
        grid_spec=grid_spec,
        out_shape=jax.ShapeDtypeStruct((nseq * t, MLA_WIDTH), F32),
        compiler_params=pltpu.CompilerParams(dimension_semantics=("arbitrary", "arbitrary"),
                                             vmem_limit_bytes=VMEM_LIMIT),
        name="attn_sample",
    )(page_table.reshape(-1), q, c_new, kp_new, ga, w_kbp, w_kb, w_vb, kn_w, ones_top, cache_ckv,
      cache_kpet)


def _outproj_kernel(or_ref, oa_ref, x_ref, w_ref, y_ref):
    y = _dot(or_ref[...].astype(BF16), w_ref[:RET_WIDTH, :])
    y = y + _dot(oa_ref[...].astype(BF16), w_ref[RET_WIDTH:, :])
    y_ref[...] = x_ref[...] + y


def _outproj(o_r, o_a, x, w_out, *, tm):
    n = x.shape[0]
    row = lambda i: (i, 0)
    return pl.pallas_call(
        _outproj_kernel,
        grid=(n // tm,),
        in_specs=[pl.BlockSpec((tm, RET_WIDTH), row), pl.BlockSpec((tm, MLA_WIDTH), row),
                  pl.BlockSpec((tm, D_MODEL), row), pl.BlockSpec(w_out.shape, lambda i: (0, 0))],
        out_specs=pl.BlockSpec((tm, D_MODEL), row),
        out_shape=jax.ShapeDtypeStruct(x.shape, F32),
        compiler_params=pltpu.CompilerParams(dimension_semantics=("arbitrary",),
                                             vmem_limit_bytes=VMEM_LIMIT),
        name="outproj",
    )(o_r, o_a, x, w_out)


def _rope_tables(pos):
    def cs(dim):
        inv = ROPE_BASE ** (-(jnp.arange(0, dim, 2, dtype=F32) / dim))
        ang = pos.astype(F32)[:, None] * inv[None, :]
        return jnp.cos(ang), jnp.sin(ang)

    cr, sr = cs(RET_DK)
    ca, sa = cs(MLA_ROPE)
    zeros = jnp.zeros((pos.shape[0], LANES - MLA_ROPE), F32)
    return (jnp.concatenate([cr, cr], axis=1), jnp.concatenate([-sr, sr], axis=1),
            jnp.concatenate([ca, ca, zeros], axis=1), jnp.concatenate([-sa, sa, zeros], axis=1))


def _decay_tables(chunk, nseq):
    log_g = jnp.log1p(-jnp.exp2(-5.0 - jnp.arange(RET_HEADS, dtype=F32)))
    i = jnp.arange(chunk, dtype=F32)
    diff = i[:, None] - i[None, :]
    intra = jnp.where(diff >= 0, jnp.exp(log_g[:, None, None] * jnp.maximum(diff, 0.0)), 0.0)
    if nseq > 1:
        same = jnp.eye(nseq, dtype=F32)
        intra = jnp.einsum("ab,hij->haibj", same, intra).reshape(RET_HEADS, nseq * chunk,
                                                                 nseq * chunk)
    q_dec = jnp.exp(log_g[None, :] * (i[:, None] + 1.0))
    k_dec = jnp.exp(log_g[None, :] * (chunk - 1.0 - i[:, None]))
    widen = lambda a: jnp.tile(jnp.repeat(a, RET_DK, axis=1), (nseq, 1))
    g = 1.0 - np.exp2(-5.0 - np.arange(RET_HEADS, dtype=np.float64))
    sdec = tuple(float(v) for v in np.exp(np.log(g) * chunk))
    return {"intra": intra, "qdec": widen(q_dec), "kdec": widen(k_dec), "sdec": sdec}


def _prep_weights(norm_w, w_in, q_a_norm_w, w_qb, qn_w, qp_w, kv_a_norm_w, kp_w, w_kb, kn_w, w_vb,
                  ret_gn_w, w_out):
    depth = w_in.shape[0]
    s = np.cumsum((512, 512, 512, 512, Q_RANK, KV_RANK, MLA_ROPE, MLA_WIDTH))
    w_in_r = jnp.concatenate(
        [w_in[:, :, :s[3]], w_in[:, :, s[6]:s[7]], w_in[:, :, s[3]:s[6]],
         jnp.zeros((depth, D_MODEL, LANES - MLA_ROPE), w_in.dtype)], axis=2).astype(BF16)
    wq = w_qb.reshape(depth, Q_RANK, MLA_HEADS, MLA_NOPE + MLA_ROPE)
    wq = jnp.pad(wq, ((0, 0), (0, 0), (0, 0), (0, QHEAD - MLA_NOPE - MLA_ROPE)))
    wq = wq.reshape(depth, Q_RANK, QCOLS).astype(BF16)
    pad_rope = lambda w: jnp.pad(w, ((0, 0), (0, LANES - MLA_ROPE)))
    layers = []
    for l in range(depth):
        layers.append({
            "norm_w": norm_w[l][None], "w_in": w_in_r[l], "q_a_norm_w": q_a_norm_w[l][None],
            "w_qb": wq[l], "qn_w": qn_w[l][None], "qp_w": pad_rope(qp_w)[l][None],
            "kv_a_norm_w": kv_a_norm_w[l][None], "kp_w": pad_rope(kp_w)[l][None],
            "w_kb": w_kb[l].astype(BF16), "kn_w": kn_w[l][None], "w_vb": w_vb[l].astype(BF16),
            "ret_gn_w": ret_gn_w[l][None], "w_out": w_out[l].astype(BF16)})
    return layers


def kernel(x_prompt, x_sample, cache_ckv, cache_kpe, state_ret, page_table, norm_w, w_in, q_a_norm_w,
           w_qb, qn_w, qp_w, kv_a_norm_w, kp_w, w_kb, kn_w, w_vb, ret_gn_w, w_out):
    b_p, t_p, _ = x_prompt.shape
    b_s, t_s, _ = x_sample.shape
    depth = w_in.shape[0]
    past = page_table.shape[1] * PAGE_SIZE
    tm_p = 512
    tm_s = 512
    ret_nseq = 8
    ret_bb = 4
    ppb = 16
    cache_kpet = jnp.swapaxes(cache_kpe, 2, 3)

    layers = _prep_weights(norm_w, w_in, q_a_norm_w, w_qb, qn_w, qp_w, kv_a_norm_w, kp_w, w_kb, kn_w,
                           w_vb, ret_gn_w, w_out)
    tabs_p = _rope_tables(jnp.arange(t_p, dtype=jnp.int32))
    tabs_s = _rope_tables(past + jnp.arange(t_s, dtype=jnp.int32))
    tabs_s = tuple(jnp.tile(a, (tm_s // t_s, 1)) for a in tabs_s)
    dec_p = _decay_tables(RET_CHUNK, 1)
    dec_s = _decay_tables(t_s, ret_nseq)

    hp = x_prompt.reshape(b_p * t_p, D_MODEL)
    hs = x_sample.reshape(b_s * t_s, D_MODEL)
    outs = {k: [] for k in ("ckv_p", "kpe_p", "ret_p", "ckv_s", "kpe_s", "ret_s")}
    for l in range(depth):
        lw = layers[l]
        qr, kr, vr, gr, ga, q, c, kpe, kcat, v = _front(hp, tabs_p, lw, tm=tm_p, emit_kv=True,
                                                        act_dtype=BF16)
        o_r, s_new = _ret_prompt(qr, kr, vr, gr, dec_p, lw["ret_gn_w"], batch=b_p, seq=t_p,
                                 bb=ret_bb)
        o_a = _attn_prompt(q, kcat, v, ga, batch=b_p, seq=t_p, tq=256, tk=256)
        hp = _outproj(o_r, o_a, hp, lw["w_out"], tm=tm_p)
        outs["ckv_p"].append(c.reshape(b_p, t_p, KV_RANK))
        outs["kpe_p"].append(kpe.reshape(b_p, t_p, MLA_ROPE))
        outs["ret_p"].append(s_new)

        qr, kr, vr, gr, ga, q, c, kpe, kp128 = _front(hs, tabs_s, lw, tm=tm_s, emit_kv=False,
                                                      act_dtype=F32)
        o_r, s_new = _ret_sample(qr, kr, vr, gr, state_ret, dec_s, lw["ret_gn_w"], layer=l, t=t_s,
                                 nseq=ret_nseq)
        o_a = _attn_sample(page_table, q, c, kp128, ga, lw["w_kb"], lw["w_vb"], lw["kn_w"],
                           cache_ckv, cache_kpet, layer=l, t=t_s, ppb=ppb)
        hs = _outproj(o_r, o_a, hs, lw["w_out"], tm=tm_s)
        outs["ckv_s"].append(c.reshape(b_s, t_s, KV_RANK))
        outs["kpe_s"].append(kpe.reshape(b_s, t_s, MLA_ROPE))
        outs["ret_s"].append(s_new)

    return (hp.reshape(b_p, t_p, D_MODEL), hs.reshape(b_s, t_s, D_MODEL),
            jnp.stack(outs["ckv_p"]), jnp.stack(outs["kpe_p"]), jnp.stack(outs["ret_p"]),
            jnp.stack(outs["ckv_s"]), jnp.stack(outs["kpe_s"]), jnp.stack(outs["ret_s"]))
```

```python
import functools

import numpy as np
import jax
import jax.numpy as jnp
from jax import lax
from jax.experimental import pallas as pl
from jax.experimental.pallas import tpu as pltpu

F32 = jnp.float32
BF16 = jnp.bfloat16

D_MODEL = 1024
PAGE_SIZE = 128
RET_HEADS = 4
RET_DK = 128
RET_DV = 128
RET_WIDTH = RET_HEADS * RET_DV
RET_CHUNK = 128
MLA_HEADS = 4
MLA_NOPE = 128
MLA_ROPE = 64
MLA_DV = 128
MLA_WIDTH = MLA_HEADS * MLA_DV
Q_RANK = 384
KV_RANK = 256
MLA_SCALE = (MLA_NOPE + MLA_ROPE) ** -0.5
LOG2E = 1.4426950408889634
ROPE_BASE = 10000.0
EPS = 1e-6
GN_EPS = 1e-5
NEG_INF = -1e30

LANES = 128
SUBLANES = 8
VMEM_LIMIT = 56 * 1024 * 1024

OFF_QR = 0
OFF_KR = 512
OFF_VR = 1024
OFF_GR = 1536
OFF_GA = 2048
OFF_QL = 2560
OFF_C = OFF_QL + Q_RANK
OFF_KPE = OFF_C + KV_RANK
IN_COLS_PAD = OFF_KPE + LANES
QHEAD = 2 * LANES
QCOLS = MLA_HEADS * QHEAD

NT_DIMS = (((1,), (1,)), ((), ()))
TN_DIMS = (((0,), (0,)), ((), ()))


def _dot(a, b):
    return jnp.dot(a, b, preferred_element_type=F32)


def _dot_nt(a, b):
    return lax.dot_general(a, b, NT_DIMS, preferred_element_type=F32)


def _dot_tn(a, b):
    return lax.dot_general(a, b, TN_DIMS, preferred_element_type=F32)


def _rms_scale(v, n):
    return lax.rsqrt(jnp.sum(v * v, axis=-1, keepdims=True) * (1.0 / n) + EPS)


def _rope64(u, cos, sin, lane):
    swapped = jnp.where(lane < MLA_ROPE // 2,
                        pltpu.roll(u, LANES - MLA_ROPE // 2, 1),
                        pltpu.roll(u, MLA_ROPE // 2, 1))
    return u * cos + swapped * sin


def _front_kernel(emit_kv, qscale, x_ref, cosr_ref, sinr_ref, cosa_ref, sina_ref, normw_ref, win_ref,
                  qanw_ref, wqb_ref, qnw_ref, qpw_ref, kvnw_ref, kpw_ref, wkb_ref, knw_ref,
                  wvb_ref, qr_ref, kr_ref, vr_ref, gr_ref, ga_ref, q_ref, c_ref, kpe_ref,
                  *kv_refs):
    x = x_ref[...]
    h = (x * _rms_scale(x, D_MODEL) * normw_ref[...]).astype(BF16)
    cosr, sinr = cosr_ref[...], sinr_ref[...]
    cosa, sina = cosa_ref[...], sina_ref[...]
    lane = lax.broadcasted_iota(jnp.int32, cosa.shape, 1)

    zq = _dot(h, win_ref[:, OFF_QR:OFF_QR + 512])
    zk = _dot(h, win_ref[:, OFF_KR:OFF_KR + 512])
    for hh in range(RET_HEADS):
        sl = slice(hh * RET_DK, (hh + 1) * RET_DK)
        qh, kh = zq[:, sl], zk[:, sl]
        qr_ref[:, sl] = (qh * cosr + pltpu.roll(qh, RET_DK // 2, 1) * sinr).astype(qr_ref.dtype)
        kr_ref[:, sl] = ((kh * cosr + pltpu.roll(kh, RET_DK // 2, 1) * sinr)
                         * (RET_DK ** -0.5)).astype(kr_ref.dtype)
    vr_ref[...] = _dot(h, win_ref[:, OFF_VR:OFF_VR + 512]).astype(vr_ref.dtype)
    zg = _dot(h, win_ref[:, OFF_GR:OFF_GR + 512])
    gr_ref[...] = (zg * jax.nn.sigmoid(zg)).astype(gr_ref.dtype)
    zg = _dot(h, win_ref[:, OFF_GA:OFF_GA + 512])
    ga_ref[...] = (zg * jax.nn.sigmoid(zg)).astype(ga_ref.dtype)

    zlow = _dot(h, win_ref[:, OFF_QL:IN_COLS_PAD])
    zql = zlow[:, :Q_RANK]
    qa = (zql * _rms_scale(zql, Q_RANK) * qanw_ref[...]).astype(BF16)
    q = _dot(qa, wqb_ref[...])
    for hh in range(MLA_HEADS):
        base = hh * QHEAD
        qn = q[:, base:base + LANES]
        qn = qn * _rms_scale(qn, MLA_NOPE) * qnw_ref[...] * qscale
        qp = q[:, base + LANES:base + QHEAD]
        u = qp * _rms_scale(qp, MLA_ROPE) * qpw_ref[...]
        qp = _rope64(u, cosa, sina, lane) * qscale
        q_ref[:, base:base + LANES] = qn.astype(q_ref.dtype)
        q_ref[:, base + LANES:base + QHEAD] = qp.astype(q_ref.dtype)

    zck = zlow[:, Q_RANK:]
    zc = zck[:, :KV_RANK]
    c = zc * _rms_scale(zc, KV_RANK) * kvnw_ref[...]
    c_ref[...] = c
    zp = zck[:, KV_RANK:]
    u = zp * _rms_scale(zp, MLA_ROPE) * kpw_ref[...]
    kp = _rope64(u, cosa, sina, lane)
    kpe_ref[...] = kp[:, :MLA_ROPE]

    if emit_kv:
        kcat_ref, v_ref = kv_refs
        cb = c.astype(BF16)
        kf = _dot(cb, wkb_ref[...])
        for hh in range(MLA_HEADS):
            kn = kf[:, hh * MLA_NOPE:(hh + 1) * MLA_NOPE]
            kn = kn * _rms_scale(kn, MLA_NOPE) * knw_ref[...]
            kcat_ref[:, hh * QHEAD:hh * QHEAD + LANES] = kn.astype(kcat_ref.dtype)
            kcat_ref[:, hh * QHEAD + LANES:(hh + 1) * QHEAD] = kp.astype(kcat_ref.dtype)
        v_ref[...] = _dot(cb, wvb_ref[...]).astype(v_ref.dtype)
    else:
        (kp128_ref,) = kv_refs
        kp128_ref[...] = kp


def _front(x, tabs, lw, *, tm, emit_kv, act_dtype):
    n = x.shape[0]
    tper = tabs[0].shape[0]
    nper = tper // tm
    row = lambda i: (i, 0)
    const = lambda i: (0, 0)
    tab_spec = pl.BlockSpec((tm, LANES), lambda i: (i % nper, 0))

    def full(a):
        return pl.BlockSpec(a.shape, const)

    weights = [lw["norm_w"], lw["w_in"], lw["q_a_norm_w"], lw["w_qb"], lw["qn_w"], lw["qp_w"],
               lw["kv_a_norm_w"], lw["kp_w"], lw["w_kb"], lw["kn_w"], lw["w_vb"]]
    in_specs = [pl.BlockSpec((tm, D_MODEL), row)] + [tab_spec] * 4 + [full(w) for w in weights]
    out_shape = [jax.ShapeDtypeStruct((n, 512), BF16)] * 3
    out_shape += [jax.ShapeDtypeStruct((n, 512), BF16)]
    out_shape += [jax.ShapeDtypeStruct((n, 512), act_dtype)]
    out_shape += [jax.ShapeDtypeStruct((n, QCOLS), act_dtype)]
    out_shape += [jax.ShapeDtypeStruct((n, KV_RANK), F32)]
    out_shape += [jax.ShapeDtypeStruct((n, MLA_ROPE), F32)]
    if emit_kv:
        out_shape += [jax.ShapeDtypeStruct((n, QCOLS), BF16)]
        out_shape += [jax.ShapeDtypeStruct((n, MLA_WIDTH), BF16)]
    else:
        out_shape += [jax.ShapeDtypeStruct((n, LANES), F32)]
    out_specs = [pl.BlockSpec((tm, s.shape[1]), row) for s in out_shape]
    return pl.pallas_call(
        functools.partial(_front_kernel, emit_kv, MLA_SCALE * LOG2E if emit_kv else MLA_SCALE),
        grid=(n // tm,),
        in_specs=in_specs,
        out_specs=out_specs,
        out_shape=out_shape,
        compiler_params=pltpu.CompilerParams(dimension_semantics=("arbitrary",),
                                             vmem_limit_bytes=VMEM_LIMIT),
        name="front_kv" if emit_kv else "front",
    )(x, *tabs, *weights)


def _ret_prompt_kernel(sdec, q_ref, k_ref, v_ref, g_ref, intra_ref, qdec_ref, kdec_ref, gnw_ref,
                       o_ref, sout_ref, s_ref):
    c = pl.program_id(1)

    @pl.when(c == 0)
    def _():
        s_ref[...] = jnp.zeros_like(s_ref)

    for b in range(q_ref.shape[0]):
        for hh in range(RET_HEADS):
            sl = slice(hh * RET_DK, (hh + 1) * RET_DK)
            q, k, v = q_ref[b, :, sl], k_ref[b, :, sl], v_ref[b, :, sl]
            s = s_ref[b, hh]
            a = _dot_nt(q, k) * intra_ref[hh]
            o = _dot(a.astype(BF16), v) + _dot(q, s.astype(BF16)) * qdec_ref[:, sl]
            kd = (k.astype(F32) * kdec_ref[:, sl]).astype(BF16)
            s_ref[b, hh] = s * sdec[hh] + _dot_tn(kd, v)
            mu = jnp.mean(o, axis=-1, keepdims=True)
            d = o - mu
            var = jnp.mean(d * d, axis=-1, keepdims=True)
            on = d * lax.rsqrt(var + GN_EPS) * gnw_ref[:, sl]
            o_ref[b, :, sl] = (on * g_ref[b, :, sl].astype(F32)).astype(o_ref.dtype)

    @pl.when(c == pl.num_programs(1) - 1)
    def _():
        sout_ref[...] = s_ref[...]


def _ret_prompt(qr, kr, vr, gr, dec, gnw, *, batch, seq, bb):
    nc = seq // RET_CHUNK
    as3d = lambda a: a.reshape(batch, seq, RET_WIDTH)
    blk = pl.BlockSpec((bb, RET_CHUNK, RET_WIDTH), lambda b, c: (b, c, 0))
    const2 = lambda b, c: (0, 0)
    o, s = pl.pallas_call(
        functools.partial(_ret_prompt_kernel, dec["sdec"]),
        grid=(batch // bb, nc),
        in_specs=[blk, blk, blk, blk,
                  pl.BlockSpec(dec["intra"].shape, lambda b, c: (0, 0, 0)),
                  pl.BlockSpec(dec["qdec"].shape, const2),
                  pl.BlockSpec(dec["kdec"].shape, const2),
                  pl.BlockSpec(gnw.shape, const2)],
        out_specs=[blk, pl.BlockSpec((bb, RET_HEADS, RET_DK, RET_DV), lambda b, c: (b, 0, 0, 0))],
        out_shape=[jax.ShapeDtypeStruct((batch, seq, RET_WIDTH), BF16),
                   jax.ShapeDtypeStruct((batch, RET_HEADS, RET_DK, RET_DV), F32)],
        scratch_shapes=[pltpu.VMEM((bb, RET_HEADS, RET_DK, RET_DV), F32)],
        compiler_params=pltpu.CompilerParams(dimension_semantics=("arbitrary", "arbitrary"),
                                             vmem_limit_bytes=VMEM_LIMIT),
        name="ret_prompt",
    )(as3d(qr), as3d(kr), as3d(vr), as3d(gr), dec["intra"], dec["qdec"], dec["kdec"], gnw)
    return o.reshape(batch * seq, RET_WIDTH), s


def _ret_sample_kernel(sdec, t, q_ref, k_ref, v_ref, g_ref, st_ref, intra_ref, qdec_ref, kdec_ref,
                       gnw_ref, o_ref, sout_ref):
    nseq = st_ref.shape[0]
    for hh in range(RET_HEADS):
        sl = slice(hh * RET_DK, (hh + 1) * RET_DK)
        q, k, v = q_ref[:, sl], k_ref[:, sl], v_ref[:, sl]
        a = _dot_nt(q, k) * intra_ref[hh]
        o_intra = _dot(a.astype(BF16), v)
        q32 = q.astype(F32)
        v32 = v.astype(F32)
        kd32 = (k.astype(F32) * kdec_ref[:, sl]).astype(BF16).astype(F32)
        qdec = qdec_ref[:, sl]
        gnw = gnw_ref[:, sl]
        for s in range(nseq):
            rows = slice(s * t, (s + 1) * t)
            s0 = st_ref[s, hh]
            o = o_intra[rows] + _dot(q32[rows], s0) * qdec[rows]
            sout_ref[s, hh] = s0 * sdec[hh] + _dot_tn(kd32[rows], v32[rows])
            mu = jnp.mean(o, axis=-1, keepdims=True)
            d = o - mu
            var = jnp.mean(d * d, axis=-1, keepdims=True)
            on = d * lax.rsqrt(var + GN_EPS) * gnw
            o_ref[rows, sl] = (on * g_ref[rows, sl].astype(F32)).astype(o_ref.dtype)


def _ret_sample(qr, kr, vr, gr, state, dec, gnw, *, layer, t, nseq):
    n = qr.shape[0]
    nb = n // t
    rows = nseq * t
    blk = pl.BlockSpec((rows, RET_WIDTH), lambda i: (i, 0))
    sblk = pl.BlockSpec((nseq, RET_HEADS, RET_DK, RET_DV), lambda i: (i, 0, 0, 0))
    sblk_in = pl.BlockSpec((None, nseq, RET_HEADS, RET_DK, RET_DV), lambda i: (layer, i, 0, 0, 0))
    const2 = lambda i: (0, 0)
    return pl.pallas_call(
        functools.partial(_ret_sample_kernel, dec["sdec"], t),
        grid=(nb // nseq,),
        in_specs=[blk, blk, blk, blk, sblk_in,
                  pl.BlockSpec(dec["intra"].shape, lambda i: (0, 0, 0)),
                  pl.BlockSpec(dec["qdec"].shape, const2),
                  pl.BlockSpec(dec["kdec"].shape, const2),
                  pl.BlockSpec(gnw.shape, const2)],
        out_specs=[blk, sblk],
        out_shape=[jax.ShapeDtypeStruct(qr.shape, BF16),
                   jax.ShapeDtypeStruct(state.shape[1:], F32)],
        compiler_params=pltpu.CompilerParams(dimension_semantics=("arbitrary",),
                                             vmem_limit_bytes=VMEM_LIMIT),
        name="ret_sample",
    )(qr, kr, vr, gr, state, dec["intra"], dec["qdec"], dec["kdec"], gnw)


def _attn_prompt_kernel(tq, tk, q_ref, k_ref, v_ref, g_ref, o_ref, m_ref, acc_ref):
    qi = pl.program_id(1)
    m_ref[...] = jnp.full(m_ref.shape, NEG_INF, F32)
    acc_ref[...] = jnp.zeros(acc_ref.shape, F32)
    row = qi * tq + lax.broadcasted_iota(jnp.int32, (tq, LANES), 0)
    col = lax.broadcasted_iota(jnp.int32, (tq, LANES), 1)
    ones = jnp.ones((tk, LANES), BF16)
    nslab = tk // LANES

    def score_slabs(hh, start, masked):
        qh = q_ref[:, hh * QHEAD:(hh + 1) * QHEAD]
        kj = k_ref[pl.ds(start, tk), hh * QHEAD:(hh + 1) * QHEAD]
        s = _dot_nt(qh, kj)
        slabs = [s[:, i * LANES:(i + 1) * LANES] for i in range(nslab)]
        if masked:
            slabs = [jnp.where(start + i * LANES + col <= row, sl, NEG_INF)
                     for i, sl in enumerate(slabs)]
        return slabs

    def block(j, masked, st):
        start = pl.multiple_of(j * tk, tk)
        for hh in range(MLA_HEADS):
            slabs = score_slabs(hh, start, masked)
            m_old = m_ref[st, hh]
            m_new = jnp.maximum(m_old, jnp.max(functools.reduce(jnp.maximum, slabs), axis=-1,
                                               keepdims=True))
            alpha = jnp.exp2(m_old - m_new)
            p = jnp.concatenate([jnp.exp2(sl - m_new) for sl in slabs], axis=1).astype(BF16)
            vj = v_ref[pl.ds(start, tk), hh * MLA_DV:(hh + 1) * MLA_DV]
            pv = _dot(p, jnp.concatenate([vj, ones], axis=1))
            acc_ref[st, hh] = acc_ref[st, hh] * jnp.concatenate([alpha, alpha], axis=1) + pv
            m_ref[st, hh] = m_new

    def body(i, carry):
        block(2 * i, False, 0)
        block(2 * i + 1, False, 1)
        return carry

    lax.fori_loop(0, qi // 2, body, 0)

    @pl.when(qi % 2 == 1)
    def _():
        block(qi - 1, False, 0)

    block(qi, True, 1)
    for hh in range(MLA_HEADS):
        m0, m1 = m_ref[0, hh], m_ref[1, hh]
        m = jnp.maximum(m0, m1)
        w0, w1 = jnp.exp2(m0 - m), jnp.exp2(m1 - m)
        acc = (acc_ref[0, hh] * jnp.concatenate([w0, w0], axis=1)
               + acc_ref[1, hh] * jnp.concatenate([w1, w1], axis=1))
        gate = g_ref[:, hh * MLA_DV:(hh + 1) * MLA_DV].astype(F32)
        o_ref[:, hh * MLA_DV:(hh + 1) * MLA_DV] = (acc[:, :MLA_DV] / acc[:, MLA_DV:]
                                                   * gate).astype(o_ref.dtype)


def _attn_prompt(q, kcat, v, ga, *, batch, seq, tq, tk):
    assert tq == tk, "one diagonal key block per query tile"
    nq = seq // tq
    return pl.pallas_call(
        functools.partial(_attn_prompt_kernel, tq, tk),
        grid=(batch, nq),
        in_specs=[pl.BlockSpec((tq, QCOLS), lambda b, i: (b * nq + i, 0)),
                  pl.BlockSpec((seq, QCOLS), lambda b, i: (b, 0)),
                  pl.BlockSpec((seq, MLA_WIDTH), lambda b, i: (b, 0)),
                  pl.BlockSpec((tq, MLA_WIDTH), lambda b, i: (b * nq + i, 0))],
        out_specs=pl.BlockSpec((tq, MLA_WIDTH), lambda b, i: (b * nq + i, 0)),
        out_shape=jax.ShapeDtypeStruct((q.shape[0], MLA_WIDTH), BF16),
        scratch_shapes=[pltpu.VMEM((2, MLA_HEADS, tq, LANES), F32),
                        pltpu.VMEM((2, MLA_HEADS, tq, 2 * MLA_DV), F32)],
        compiler_params=pltpu.CompilerParams(dimension_semantics=("arbitrary", "arbitrary"),
                                             vmem_limit_bytes=VMEM_LIMIT),
        name="attn_prompt",
    )(q, kcat, v, ga)


def _col_from_row(row_vec, eye):
    return jnp.sum(jnp.where(eye, jnp.broadcast_to(row_vec, eye.shape), 0.0), axis=1,
                   keepdims=True)


def _page_copies(layer, ppb, pt_ref, ckv_hbm, kpet_hbm, cbuf, kpbuf, sems, step, slot):
    copies = []
    for j in range(ppb):
        page = pt_ref[step * ppb + j]
        rows = pl.ds(j * PAGE_SIZE, PAGE_SIZE)
        copies.append(pltpu.make_async_copy(ckv_hbm.at[layer, page], cbuf.at[slot, rows],
                                            sems.at[slot, 0]))
        copies.append(pltpu.make_async_copy(kpet_hbm.at[layer, page],
                                            kpbuf.at[slot, pl.ds(0, MLA_ROPE), rows],
                                            sems.at[slot, 1]))
    return copies


def _attn_sample_kernel(layer, ppb, t, nblk, pt_ref, q_ref, cnew_ref, kpnew_ref, g_ref, wkbp_ref,
                        wkb_ref, wvb_ref, knw_ref, ones_ref, ckv_hbm, kpet_hbm, o_ref,
                        cbuf, kpbuf, sems, cb_ref, sn_ref, r2_ref, m_ref, l_ref, acc_ref, qtil_ref,
                        b2t_ref):
    step = pl.program_id(0)
    nstep = pl.num_programs(0) - 1
    slot = step % 2
    qbuf = (step // nblk) % 2
    group = MLA_HEADS * t
    nsub = LANES // group
    rows = cbuf.shape[1]
    sub = rows // nsub
    copies = functools.partial(_page_copies, layer, ppb, pt_ref, ckv_hbm, kpet_hbm, cbuf, kpbuf,
                               sems)

    ri = lax.broadcasted_iota(jnp.int32, (LANES, LANES), 0)
    ci = lax.broadcasted_iota(jnp.int32, (LANES, LANES), 1)
    eye = ri == ci

    def build_operands(buf):
        q = q_ref[...]
        qn = jnp.concatenate([q[:, hh * QHEAD:hh * QHEAD + LANES] for hh in range(MLA_HEADS)],
                             axis=1) * jnp.concatenate([knw_ref[...]] * MLA_HEADS, axis=1)
        qn = jnp.concatenate([qn] * (LANES // t), axis=0)
        rh = (lax.broadcasted_iota(jnp.int32, qn.shape, 0) % group) // t
        lh = lax.broadcasted_iota(jnp.int32, qn.shape, 1) // MLA_NOPE
        qt = jnp.where(rh == lh, qn, 0.0).astype(BF16)
        qtil = _dot_nt(wkb_ref[...], qt)
        lane_g = lax.broadcasted_iota(jnp.int32, qtil.shape, 1) // group
        qp = jnp.concatenate([q[:, hh * QHEAD + LANES:(hh + 1) * QHEAD] for hh in range(MLA_HEADS)],
                             axis=0)
        qp = jnp.concatenate([qp] * nsub, axis=0)
        row_g = lax.broadcasted_iota(jnp.int32, qp.shape, 0) // group
        for g in range(nsub):
            qtil_ref[buf, g] = jnp.where(lane_g == g, qtil, 0.0).astype(BF16)
            b2t_ref[buf, g, LANES:, :] = jnp.concatenate(
                [jnp.zeros(qp.shape, F32), jnp.where(row_g == g, qp, 0.0)], axis=1).astype(BF16)

    @pl.when(step == 0)
    def _():
        kpbuf[:, MLA_ROPE:, :] = jnp.zeros((2, LANES - MLA_ROPE, rows), F32)
        for buf in range(2):
            for g in range(nsub):
                b2t_ref[buf, g, :LANES, :] = ones_ref[g]
        sn_ref[1] = jnp.zeros(sn_ref.shape[1:], F32)
        r2_ref[1] = jnp.zeros(r2_ref.shape[1:], F32)
        cb_ref[1] = jnp.zeros(cb_ref.shape[1:], BF16)
        m_ref[...] = jnp.full(m_ref.shape, NEG_INF, F32)
        l_ref[...] = jnp.zeros(l_ref.shape, F32)
        acc_ref[...] = jnp.zeros(acc_ref.shape, F32)
        build_operands(0)
        for cp in copies(step, slot):
            cp.start()

    def scores(c_bf, kp_pad, buf, g):
        kf = _dot(c_bf, wkbp_ref[...])
        sq = kf * kf
        fold = (sq[:, 0:LANES] + sq[:, LANES:2 * LANES]) + (sq[:, 2 * LANES:3 * LANES]
                                                            + sq[:, 3 * LANES:4 * LANES])
        lhs2 = jnp.concatenate([fold.astype(BF16), kp_pad.astype(BF16)], axis=1)
        return _dot(c_bf, qtil_ref[buf, g]), _dot_nt(lhs2, b2t_ref[buf, g])

    def softmax_step(sn, r2, valid):
        s = sn * lax.rsqrt(r2[:, :LANES] * (1.0 / MLA_NOPE) + EPS) + r2[:, LANES:]
        if valid is not None:
            s = jnp.where(valid, s, NEG_INF)
        m_old = m_ref[0:1, :]
        m_new = jnp.maximum(m_old, jnp.max(s, axis=0, keepdims=True))
        alpha = jnp.exp(m_old - m_new)
        p = jnp.exp(s - m_new)
        if valid is not None:
            p = jnp.where(valid, p, 0.0)
        l_ref[0:1, :] = l_ref[0:1, :] * alpha + jnp.sum(p, axis=0, keepdims=True)
        m_ref[0:1, :] = m_new
        return p, _col_from_row(alpha, eye)

    nxt = jnp.minimum(step + 1, nstep - 1)

    def stages(cur):
        prv = 1 - cur
        for cp in copies(jnp.minimum(step, nstep - 1), cur):
            cp.wait()
        for cp in copies(nxt, prv):
            cp.start()

        p, alpha_col = softmax_step(sn_ref[prv], r2_ref[prv], None)
        lane_g = lax.broadcasted_iota(jnp.int32, p.shape, 1) // group
        p_rows = jnp.concatenate([jnp.where(lane_g == g, p, 0.0) for g in range(nsub)], axis=0)
        acc_ref[...] = acc_ref[...] * alpha_col + _dot_tn(p_rows.astype(BF16), cb_ref[prv])

        sn = r2 = None
        for g in range(nsub):
            rs = pl.ds(g * sub, sub)
            c_g = cbuf[cur, rs, :].astype(BF16)
            cb_ref[cur, rs, :] = c_g
            sn_g, r2_g = scores(c_g, kpbuf[cur, :, rs].T, qbuf, g)
            sn = sn_g if sn is None else sn + sn_g
            r2 = r2_g if r2 is None else r2 + r2_g
        sn_ref[cur] = sn
        r2_ref[cur] = r2

    for par in range(2):
        pl.when(slot == par)(functools.partial(stages, par))

    @pl.when(step % nblk == 0)
    def _():
        @pl.when(step > 0)
        def _():
            m_col = _col_from_row(m_ref[0:1, :], eye)
            l_col = _col_from_row(l_ref[0:1, :], eye)
            parts = [slice(g * group, (g + 1) * group) for g in range(nsub)]
            m_fin = functools.reduce(jnp.maximum, [m_col[sl] for sl in parts])
            w = [jnp.exp(m_col[sl] - m_fin) for sl in parts]
            l_fin = sum(w[g] * l_col[parts[g]] for g in range(nsub))
            lat = sum(w[g] * acc_ref[parts[g], :] for g in range(nsub)) / l_fin
            full = _dot(lat.astype(BF16), wvb_ref[...])
            o = jnp.concatenate([full[hh * t:(hh + 1) * t, hh * MLA_DV:(hh + 1) * MLA_DV]
                                 for hh in range(MLA_HEADS)], axis=1)
            o_ref[...] = (o * g_ref[...]).astype(o_ref.dtype)

        @pl.when(step < nstep)
        def _():
            m_ref[...] = jnp.full(m_ref.shape, NEG_INF, F32)
            l_ref[...] = jnp.zeros(l_ref.shape, F32)
            pad = 2 * SUBLANES - t
            c_new = jnp.concatenate([cnew_ref[...], jnp.zeros((pad, KV_RANK), F32)],
                                    axis=0).astype(BF16)
            kp_new = jnp.concatenate([kpnew_ref[...], jnp.zeros((pad, LANES), F32)], axis=0)
            r = lax.broadcasted_iota(jnp.int32, (2 * SUBLANES, LANES), 0)
            j = lax.broadcasted_iota(jnp.int32, (2 * SUBLANES, LANES), 1)
            sn0, r20 = scores(c_new, kp_new, qbuf, 0)
            p0, _ = softmax_step(sn0, r20, (r < t) & (r <= j % t) & (j < group))
            acc_ref[...] = _dot_tn(p0.astype(BF16), c_new)

    @pl.when(jnp.logical_and((step + 1) % nblk == 0, step + 1 < nstep))
    def _():
        build_operands(1 - qbuf)

    @pl.when(step == nstep)
    def _():
        for cp in copies(nxt, 1 - slot):
            cp.wait()


def _attn_sample(page_table, q, c_new, kp_new, ga, w_kb, w_vb, kn_w, cache_ckv, cache_kpet, *,
                 layer, t, ppb):
    nseq, npages = page_table.shape
    nblk = npages // ppb
    rows = ppb * PAGE_SIZE
    group = MLA_HEADS * t
    nsub = LANES // group
    fold = MLA_NOPE // nsub
    w_kbp = w_kb.reshape(KV_RANK, MLA_HEADS, nsub, fold).transpose(0, 2, 1, 3).reshape(w_kb.shape)
    j = np.arange(LANES)
    d = np.arange(2 * LANES)
    ones_top = ((d[None, None, :] < LANES)
                & (d[None, None, :] // fold == ((j % group) // t)[None, :, None])
                & (j[None, :, None] // group == np.arange(nsub)[:, None, None]))
    ones_top = jnp.asarray(ones_top, BF16)
    assert nblk >= 2, "the projection / softmax pipeline needs at least two blocks per sequence"
    nstep = nseq * nblk
    seq_next = lambda s, pt: (jnp.minimum((s + 1) // nblk, nseq - 1), 0)
    seq_now = lambda s, pt: (jnp.minimum(s // nblk, nseq - 1), 0)
    seq_prev = lambda s, pt: (jnp.maximum(s - 1, 0) // nblk, 0)
    const = lambda s, pt: (0, 0)
    sub = rows // nsub
    grid_spec = pltpu.PrefetchScalarGridSpec(
        num_scalar_prefetch=1,
        grid=(nstep + 1,),
        in_specs=[pl.BlockSpec((t, QCOLS), seq_next),
                  pl.BlockSpec((t, KV_RANK), seq_now),
                  pl.BlockSpec((t, LANES), seq_now),
                  pl.BlockSpec((t, MLA_WIDTH), seq_prev),
                  pl.BlockSpec(w_kbp.shape, const),
                  pl.BlockSpec(w_kb.shape, const),
                  pl.BlockSpec(w_vb.shape, const),
                  pl.BlockSpec(kn_w.shape, const),
                  pl.BlockSpec(ones_top.shape, lambda s, pt: (0, 0, 0)),
                  pl.BlockSpec(memory_space=pl.ANY),
                  pl.BlockSpec(memory_space=pl.ANY)],
        out_specs=pl.BlockSpec((t, MLA_WIDTH), seq_prev),
        scratch_shapes=[pltpu.VMEM((2, rows, KV_RANK), F32),
                        pltpu.VMEM((2, LANES, rows), F32),
                        pltpu.SemaphoreType.DMA((2, 2)),
                        pltpu.VMEM((2, rows, KV_RANK), BF16),
                        pltpu.VMEM((2, sub, LANES), F32),
                        pltpu.VMEM((2, sub, 2 * LANES), F32),
                        pltpu.VMEM((SUBLANES, LANES), F32),
                        pltpu.VMEM((SUBLANES, LANES), F32),
                        pltpu.VMEM((LANES, KV_RANK), F32),
                        pltpu.VMEM((2, nsub, KV_RANK, LANES), BF16),
                        pltpu.VMEM((2, nsub, 2 * LANES, 2 * LANES), BF16)])
    return pl.pallas_call(
        functools.partial(_attn_sample_kernel, layer, ppb, t, nblk),
        grid_spec=grid_spec,
        out_shape=jax.ShapeDtypeStruct((nseq * t, MLA_WIDTH), F32),
        compiler_params=pltpu.CompilerParams(dimension_semantics=("arbitrary",),
                                             vmem_limit_bytes=VMEM_LIMIT),
        name="attn_sample",
    )(page_table.reshape(-1), q, c_new, kp_new, ga, w_kbp, w_kb, w_vb, kn_w, ones_top, cache_ckv,
      cache_kpet)


def _outproj_kernel(or_ref, oa_ref, x_ref, w_ref, y_ref):
    y = _dot(or_ref[...].astype(BF16), w_ref[:RET_WIDTH, :])
    y = y + _dot(oa_ref[...].astype(BF16), w_ref[RET_WIDTH:, :])
    y_ref[...] = x_ref[...] + y


def _outproj(o_r, o_a, x, w_out, *, tm):
    n = x.shape[0]
    row = lambda i: (i, 0)
    return pl.pallas_call(
        _outproj_kernel,
        grid=(n // tm,),
        in_specs=[pl.BlockSpec((tm, RET_WIDTH), row), pl.BlockSpec((tm, MLA_WIDTH), row),
                  pl.BlockSpec((tm, D_MODEL), row), pl.BlockSpec(w_out.shape, lambda i: (0, 0))],
        out_specs=pl.BlockSpec((tm, D_MODEL), row),
        out_shape=jax.ShapeDtypeStruct(x.shape, F32),
        compiler_params=pltpu.CompilerParams(dimension_semantics=("arbitrary",),
                                             vmem_limit_bytes=VMEM_LIMIT),
        name="outproj",
    )(o_r, o_a, x, w_out)


def _rope_tables(pos):
    def cs(dim):
        inv = ROPE_BASE ** (-(jnp.arange(0, dim, 2, dtype=F32) / dim))
        ang = pos.astype(F32)[:, None] * inv[None, :]
        return jnp.cos(ang), jnp.sin(ang)

    cr, sr = cs(RET_DK)
    ca, sa = cs(MLA_ROPE)
    zeros = jnp.zeros((pos.shape[0], LANES - MLA_ROPE), F32)
    return (jnp.concatenate([cr, cr], axis=1), jnp.concatenate([-sr, sr], axis=1),
            jnp.concatenate([ca, ca, zeros], axis=1), jnp.concatenate([-sa, sa, zeros], axis=1))


def _decay_tables(chunk, nseq):
    log_g = jnp.log1p(-jnp.exp2(-5.0 - jnp.arange(RET_HEADS, dtype=F32)))
    i = jnp.arange(chunk, dtype=F32)
    diff = i[:, None] - i[None, :]
    intra = jnp.where(diff >= 0, jnp.exp(log_g[:, None, None] * jnp.maximum(diff, 0.0)), 0.0)
    if nseq > 1:
        same = jnp.eye(nseq, dtype=F32)
        intra = jnp.einsum("ab,hij->haibj", same, intra).reshape(RET_HEADS, nseq * chunk,
                                                                 nseq * chunk)
    q_dec = jnp.exp(log_g[None, :] * (i[:, None] + 1.0))
    k_dec = jnp.exp(log_g[None, :] * (chunk - 1.0 - i[:, None]))
    widen = lambda a: jnp.tile(jnp.repeat(a, RET_DK, axis=1), (nseq, 1))
    g = 1.0 - np.exp2(-5.0 - np.arange(RET_HEADS, dtype=np.float64))
    sdec = tuple(float(v) for v in np.exp(np.log(g) * chunk))
    return {"intra": intra, "qdec": widen(q_dec), "kdec": widen(k_dec), "sdec": sdec}


def _prep_weights(norm_w, w_in, q_a_norm_w, w_qb, qn_w, qp_w, kv_a_norm_w, kp_w, w_kb, kn_w, w_vb,
                  ret_gn_w, w_out):
    depth = w_in.shape[0]
    s = np.cumsum((512, 512, 512, 512, Q_RANK, KV_RANK, MLA_ROPE, MLA_WIDTH))
    w_in_r = jnp.concatenate(
        [w_in[:, :, :s[3]], w_in[:, :, s[6]:s[7]], w_in[:, :, s[3]:s[6]],
         jnp.zeros((depth, D_MODEL, LANES - MLA_ROPE), w_in.dtype)], axis=2).astype(BF16)
    wq = w_qb.reshape(depth, Q_RANK, MLA_HEADS, MLA_NOPE + MLA_ROPE)
    wq = jnp.pad(wq, ((0, 0), (0, 0), (0, 0), (0, QHEAD - MLA_NOPE - MLA_ROPE)))
    wq = wq.reshape(depth, Q_RANK, QCOLS).astype(BF16)
    pad_rope = lambda w: jnp.pad(w, ((0, 0), (0, LANES - MLA_ROPE)))
    layers = []
    for l in range(depth):
        layers.append({
            "norm_w": norm_w[l][None], "w_in": w_in_r[l], "q_a_norm_w": q_a_norm_w[l][None],
            "w_qb": wq[l], "qn_w": qn_w[l][None], "qp_w": pad_rope(qp_w)[l][None],
            "kv_a_norm_w": kv_a_norm_w[l][None], "kp_w": pad_rope(kp_w)[l][None],
            "w_kb": w_kb[l].astype(BF16), "kn_w": kn_w[l][None], "w_vb": w_vb[l].astype(BF16),
            "ret_gn_w": ret_gn_w[l][None], "w_out": w_out[l].astype(BF16)})
    return layers


def kernel(x_prompt, x_sample, cache_ckv, cache_kpe, state_ret, page_table, norm_w, w_in, q_a_norm_w,
           w_qb, qn_w, qp_w, kv_a_norm_w, kp_w, w_kb, kn_w, w_vb, ret_gn_w, w_out):
    b_p, t_p, _ = x_prompt.shape
    b_s, t_s, _ = x_sample.shape
    depth = w_in.shape[0]
    past = page_table.shape[1] * PAGE_SIZE
    tm_p = 512
    tm_s = 512
    ret_nseq = 8
    ret_bb = 4
    ppb = 16
    cache_kpet = jnp.swapaxes(cache_kpe, 2, 3)

    layers = _prep_weights(norm_w, w_in, q_a_norm_w, w_qb, qn_w, qp_w, kv_a_norm_w, kp_w, w_kb, kn_w,
                           w_vb, ret_gn_w, w_out)
    tabs_p = _rope_tables(jnp.arange(t_p, dtype=jnp.int32))
    tabs_s = _rope_tables(past + jnp.arange(t_s, dtype=jnp.int32))
    tabs_s = tuple(jnp.tile(a, (tm_s // t_s, 1)) for a in tabs_s)
    dec_p = _decay_tables(RET_CHUNK, 1)
    dec_s = _decay_tables(t_s, ret_nseq)

    hp = x_prompt.reshape(b_p * t_p, D_MODEL)
    hs = x_sample.reshape(b_s * t_s, D_MODEL)
    outs = {k: [] for k in ("ckv_p", "kpe_p", "ret_p", "ckv_s", "kpe_s", "ret_s")}
    for l in range(depth):
        lw = layers[l]
        qr, kr, vr, gr, ga, q, c, kpe, kcat, v = _front(hp, tabs_p, lw, tm=tm_p, emit_kv=True,
                                                        act_dtype=BF16)
        o_r, s_new = _ret_prompt(qr, kr, vr, gr, dec_p, lw["ret_gn_w"], batch=b_p, seq=t_p,
                                 bb=ret_bb)
        o_a = _attn_prompt(q, kcat, v, ga, batch=b_p, seq=t_p, tq=256, tk=256)
        hp = _outproj(o_r, o_a, hp, lw["w_out"], tm=tm_p)
        outs["ckv_p"].append(c.reshape(b_p, t_p, KV_RANK))
        outs["kpe_p"].append(kpe.reshape(b_p, t_p, MLA_ROPE))
        outs["ret_p"].append(s_new)

        qr, kr, vr, gr, ga, q, c, kpe, kp128 = _front(hs, tabs_s, lw, tm=tm_s, emit_kv=False,
                                                      act_dtype=F32)
        o_r, s_new = _ret_sample(qr, kr, vr, gr, state_ret, dec_s, lw["ret_gn_w"], layer=l, t=t_s,
                                 nseq=ret_nseq)
        o_a = _attn_sample(page_table, q, c, kp128, ga, lw["w_kb"], lw["w_vb"], lw["kn_w"],
                           cache_ckv, cache_kpet, layer=l, t=t_s, ppb=ppb)
        hs = _outproj(o_r, o_a, hs, lw["w_out"], tm=tm_s)
        outs["ckv_s"].append(c.reshape(b_s, t_s, KV_RANK))
        outs["kpe_s"].append(kpe.reshape(b_s, t_s, MLA_ROPE))
        outs["ret_s"].append(s_new)

    return (hp.reshape(b_p, t_p, D_MODEL), hs.reshape(b_s, t_s, D_MODEL),
            jnp.stack(outs["ckv_p"]), jnp.stack(outs["kpe_p"]), jnp.stack(outs["ret_p"]),
            jnp.stack(outs["ckv_s"]), jnp.stack(outs["kpe_s"]), jnp.stack(outs["ret_s"]))
```

```python
import functools

import numpy as np
import jax
import jax.numpy as jnp
from jax import lax
from jax.experimental import pallas as pl
from jax.experimental.pallas import tpu as pltpu

F32 = jnp.float32
BF16 = jnp.bfloat16

D_MODEL = 1024
PAGE_SIZE = 128
RET_HEADS = 4
RET_DK = 128
RET_DV = 128
RET_WIDTH = RET_HEADS * RET_DV
RET_CHUNK = 128
MLA_HEADS = 4
MLA_NOPE = 128
MLA_ROPE = 64
MLA_DV = 128
MLA_WIDTH = MLA_HEADS * MLA_DV
Q_RANK = 384
KV_RANK = 256
MLA_SCALE = (MLA_NOPE + MLA_ROPE) ** -0.5
LOG2E = 1.4426950408889634
ROPE_BASE = 10000.0
EPS = 1e-6
GN_EPS = 1e-5
NEG_INF = -1e30

LANES = 128
SUBLANES = 8
VMEM_LIMIT = 56 * 1024 * 1024

OFF_QR = 0
OFF_KR = 512
OFF_VR = 1024
OFF_GR = 1536
OFF_GA = 2048
OFF_QL = 2560
OFF_C = OFF_QL + Q_RANK
OFF_KPE = OFF_C + KV_RANK
IN_COLS_PAD = OFF_KPE + LANES
QHEAD = 2 * LANES
QCOLS = MLA_HEADS * QHEAD

NT_DIMS = (((1,), (1,)), ((), ()))
TN_DIMS = (((0,), (0,)), ((), ()))


def _dot(a, b):
    return jnp.dot(a, b, preferred_element_type=F32)


def _dot_nt(a, b):
    return lax.dot_general(a, b, NT_DIMS, preferred_element_type=F32)


def _dot_tn(a, b):
    return lax.dot_general(a, b, TN_DIMS, preferred_element_type=F32)


def _rms_scale(v, n):
    return lax.rsqrt(jnp.sum(v * v, axis=-1, keepdims=True) * (1.0 / n) + EPS)


def _rope64(u, cos, sin, lane):
    swapped = jnp.where(lane < MLA_ROPE // 2,
                        pltpu.roll(u, LANES - MLA_ROPE // 2, 1),
                        pltpu.roll(u, MLA_ROPE // 2, 1))
    return u * cos + swapped * sin


def _front_kernel(emit_kv, qscale, x_ref, cosr_ref, sinr_ref, cosa_ref, sina_ref, normw_ref, win_ref,
                  qanw_ref, wqb_ref, qnw_ref, qpw_ref, kvnw_ref, kpw_ref, wkb_ref, knw_ref,
                  wvb_ref, qr_ref, kr_ref, vr_ref, gr_ref, ga_ref, q_ref, c_ref, kpe_ref,
                  *kv_refs):
    x = x_ref[...]
    h = (x * _rms_scale(x, D_MODEL) * normw_ref[...]).astype(BF16)
    cosr, sinr = cosr_ref[...], sinr_ref[...]
    cosa, sina = cosa_ref[...], sina_ref[...]
    lane = lax.broadcasted_iota(jnp.int32, cosa.shape, 1)

    half = 2 * RET_DK
    for c0 in range(0, RET_WIDTH, half):
        cs = slice(c0, c0 + half)
        zq = _dot(h, win_ref[:, OFF_QR + c0:OFF_QR + c0 + half])
        zk = _dot(h, win_ref[:, OFF_KR + c0:OFF_KR + c0 + half])
        for d0 in range(0, half, RET_DK):
            sl = slice(c0 + d0, c0 + d0 + RET_DK)
            qh, kh = zq[:, d0:d0 + RET_DK], zk[:, d0:d0 + RET_DK]
            qr_ref[:, sl] = (qh * cosr + pltpu.roll(qh, RET_DK // 2, 1) * sinr).astype(qr_ref.dtype)
            kr_ref[:, sl] = ((kh * cosr + pltpu.roll(kh, RET_DK // 2, 1) * sinr)
                             * (RET_DK ** -0.5)).astype(kr_ref.dtype)
        vr_ref[:, cs] = _dot(h, win_ref[:, OFF_VR + c0:OFF_VR + c0 + half]).astype(vr_ref.dtype)
        zg = _dot(h, win_ref[:, OFF_GR + c0:OFF_GR + c0 + half])
        gr_ref[:, cs] = (zg * jax.nn.sigmoid(zg)).astype(gr_ref.dtype)
        zg = _dot(h, win_ref[:, OFF_GA + c0:OFF_GA + c0 + half])
        ga_ref[:, cs] = (zg * jax.nn.sigmoid(zg)).astype(ga_ref.dtype)

    zql = _dot(h, win_ref[:, OFF_QL:OFF_QL + Q_RANK])
    qa = (zql * _rms_scale(zql, Q_RANK) * qanw_ref[...]).astype(BF16)
    q = _dot(qa, wqb_ref[...])
    for hh in range(MLA_HEADS):
        base = hh * QHEAD
        qn = q[:, base:base + LANES]
        qn = qn * _rms_scale(qn, MLA_NOPE) * qnw_ref[...] * qscale
        qp = q[:, base + LANES:base + QHEAD]
        u = qp * _rms_scale(qp, MLA_ROPE) * qpw_ref[...]
        qp = _rope64(u, cosa, sina, lane) * qscale
        q_ref[:, base:base + LANES] = qn.astype(q_ref.dtype)
        q_ref[:, base + LANES:base + QHEAD] = qp.astype(q_ref.dtype)

    zck = _dot(h, win_ref[:, OFF_C:OFF_C + KV_RANK + LANES])
    zc = zck[:, :KV_RANK]
    c = zc * _rms_scale(zc, KV_RANK) * kvnw_ref[...]
    c_ref[...] = c
    zp = zck[:, KV_RANK:]
    u = zp * _rms_scale(zp, MLA_ROPE) * kpw_ref[...]
    kp = _rope64(u, cosa, sina, lane)
    kpe_ref[...] = kp[:, :MLA_ROPE]

    if emit_kv:
        kcat_ref, v_ref = kv_refs
        cb = c.astype(BF16)
        kf = _dot(cb, wkb_ref[...])
        for hh in range(MLA_HEADS):
            kn = kf[:, hh * MLA_NOPE:(hh + 1) * MLA_NOPE]
            kn = kn * _rms_scale(kn, MLA_NOPE) * knw_ref[...]
            kcat_ref[:, hh * QHEAD:hh * QHEAD + LANES] = kn.astype(kcat_ref.dtype)
            kcat_ref[:, hh * QHEAD + LANES:(hh + 1) * QHEAD] = kp.astype(kcat_ref.dtype)
        v_ref[...] = _dot(cb, wvb_ref[...]).astype(v_ref.dtype)
    else:
        (kp128_ref,) = kv_refs
        kp128_ref[...] = kp


def _front(x, tabs, lw, *, tm, emit_kv, act_dtype):
    n = x.shape[0]
    tper = tabs[0].shape[0]
    nper = tper // tm
    row = lambda i: (i, 0)
    const = lambda i: (0, 0)
    tab_spec = pl.BlockSpec((tm, LANES), lambda i: (i % nper, 0))

    def full(a):
        return pl.BlockSpec(a.shape, const)

    weights = [lw["norm_w"], lw["w_in"], lw["q_a_norm_w"], lw["w_qb"], lw["qn_w"], lw["qp_w"],
               lw["kv_a_norm_w"], lw["kp_w"], lw["w_kb"], lw["kn_w"], lw["w_vb"]]
    in_specs = [pl.BlockSpec((tm, D_MODEL), row)] + [tab_spec] * 4 + [full(w) for w in weights]
    out_shape = [jax.ShapeDtypeStruct((n, 512), BF16)] * 3
    out_shape += [jax.ShapeDtypeStruct((n, 512), BF16)]
    out_shape += [jax.ShapeDtypeStruct((n, 512), act_dtype)]
    out_shape += [jax.ShapeDtypeStruct((n, QCOLS), act_dtype)]
    out_shape += [jax.ShapeDtypeStruct((n, KV_RANK), F32)]
    out_shape += [jax.ShapeDtypeStruct((n, MLA_ROPE), F32)]
    if emit_kv:
        out_shape += [jax.ShapeDtypeStruct((n, QCOLS), BF16)]
        out_shape += [jax.ShapeDtypeStruct((n, MLA_WIDTH), BF16)]
    else:
        out_shape += [jax.ShapeDtypeStruct((n, LANES), F32)]
    out_specs = [pl.BlockSpec((tm, s.shape[1]), row) for s in out_shape]
    return pl.pallas_call(
        functools.partial(_front_kernel, emit_kv, MLA_SCALE * LOG2E if emit_kv else MLA_SCALE),
        grid=(n // tm,),
        in_specs=in_specs,
        out_specs=out_specs,
        out_shape=out_shape,
        compiler_params=pltpu.CompilerParams(dimension_semantics=("arbitrary",),
                                             vmem_limit_bytes=VMEM_LIMIT),
        name="front_kv" if emit_kv else "front",
    )(x, *tabs, *weights)


def _ret_prompt_kernel(sdec, q_ref, k_ref, v_ref, g_ref, intra_ref, qdec_ref, kdec_ref, gnw_ref,
                       o_ref, sout_ref, s_ref):
    c = pl.program_id(1)

    @pl.when(c == 0)
    def _():
        s_ref[...] = jnp.zeros_like(s_ref)

    for b in range(q_ref.shape[0]):
        for hh in range(RET_HEADS):
            sl = slice(hh * RET_DK, (hh + 1) * RET_DK)
            q, k, v = q_ref[b, :, sl], k_ref[b, :, sl], v_ref[b, :, sl]
            s = s_ref[b, hh]
            a = _dot_nt(q, k) * intra_ref[hh]
            o = _dot(a.astype(BF16), v) + _dot(q, s.astype(BF16)) * qdec_ref[:, sl]
            kd = (k.astype(F32) * kdec_ref[:, sl]).astype(BF16)
            s_ref[b, hh] = s * sdec[hh] + _dot_tn(kd, v)
            mu = jnp.mean(o, axis=-1, keepdims=True)
            d = o - mu
            var = jnp.mean(d * d, axis=-1, keepdims=True)
            on = d * lax.rsqrt(var + GN_EPS) * gnw_ref[:, sl]
            o_ref[b, :, sl] = (on * g_ref[b, :, sl].astype(F32)).astype(o_ref.dtype)

    @pl.when(c == pl.num_programs(1) - 1)
    def _():
        sout_ref[...] = s_ref[...]


def _ret_prompt(qr, kr, vr, gr, dec, gnw, *, batch, seq, bb):
    nc = seq // RET_CHUNK
    as3d = lambda a: a.reshape(batch, seq, RET_WIDTH)
    blk = pl.BlockSpec((bb, RET_CHUNK, RET_WIDTH), lambda b, c: (b, c, 0))
    const2 = lambda b, c: (0, 0)
    o, s = pl.pallas_call(
        functools.partial(_ret_prompt_kernel, dec["sdec"]),
        grid=(batch // bb, nc),
        in_specs=[blk, blk, blk, blk,
                  pl.BlockSpec(dec["intra"].shape, lambda b, c: (0, 0, 0)),
                  pl.BlockSpec(dec["qdec"].shape, const2),
                  pl.BlockSpec(dec["kdec"].shape, const2),
                  pl.BlockSpec(gnw.shape, const2)],
        out_specs=[blk, pl.BlockSpec((bb, RET_HEADS, RET_DK, RET_DV), lambda b, c: (b, 0, 0, 0))],
        out_shape=[jax.ShapeDtypeStruct((batch, seq, RET_WIDTH), BF16),
                   jax.ShapeDtypeStruct((batch, RET_HEADS, RET_DK, RET_DV), F32)],
        scratch_shapes=[pltpu.VMEM((bb, RET_HEADS, RET_DK, RET_DV), F32)],
        compiler_params=pltpu.CompilerParams(dimension_semantics=("arbitrary", "arbitrary"),
                                             vmem_limit_bytes=VMEM_LIMIT),
        name="ret_prompt",
    )(as3d(qr), as3d(kr), as3d(vr), as3d(gr), dec["intra"], dec["qdec"], dec["kdec"], gnw)
    return o.reshape(batch * seq, RET_WIDTH), s


def _ret_sample_kernel(sdec, t, q_ref, k_ref, v_ref, g_ref, st_ref, intra_ref, qdec_ref, kdec_ref,
                       gnw_ref, o_ref, sout_ref):
    nseq = st_ref.shape[0]
    for hh in range(RET_HEADS):
        sl = slice(hh * RET_DK, (hh + 1) * RET_DK)
        q, k, v = q_ref[:, sl], k_ref[:, sl], v_ref[:, sl]
        a = _dot_nt(q, k) * intra_ref[hh]
        o_intra = _dot(a.astype(BF16), v)
        q32 = q.astype(F32)
        v32 = v.astype(F32)
        kd32 = (k.astype(F32) * kdec_ref[:, sl]).astype(BF16).astype(F32)
        qdec = qdec_ref[:, sl]
        gnw = gnw_ref[:, sl]
        for s in range(nseq):
            rows = slice(s * t, (s + 1) * t)
            s0 = st_ref[s, hh]
            o = o_intra[rows] + _dot(q32[rows], s0) * qdec[rows]
            sout_ref[s, hh] = s0 * sdec[hh] + _dot_tn(kd32[rows], v32[rows])
            mu = jnp.mean(o, axis=-1, keepdims=True)
            d = o - mu
            var = jnp.mean(d * d, axis=-1, keepdims=True)
            on = d * lax.rsqrt(var + GN_EPS) * gnw
            o_ref[rows, sl] = (on * g_ref[rows, sl].astype(F32)).astype(o_ref.dtype)


def _ret_sample(qr, kr, vr, gr, state, dec, gnw, *, layer, t, nseq):
    n = qr.shape[0]
    nb = n // t
    rows = nseq * t
    blk = pl.BlockSpec((rows, RET_WIDTH), lambda i: (i, 0))
    sblk = pl.BlockSpec((nseq, RET_HEADS, RET_DK, RET_DV), lambda i: (i, 0, 0, 0))
    sblk_in = pl.BlockSpec((None, nseq, RET_HEADS, RET_DK, RET_DV), lambda i: (layer, i, 0, 0, 0))
    const2 = lambda i: (0, 0)
    return pl.pallas_call(
        functools.partial(_ret_sample_kernel, dec["sdec"], t),
        grid=(nb // nseq,),
        in_specs=[blk, blk, blk, blk, sblk_in,
                  pl.BlockSpec(dec["intra"].shape, lambda i: (0, 0, 0)),
                  pl.BlockSpec(dec["qdec"].shape, const2),
                  pl.BlockSpec(dec["kdec"].shape, const2),
                  pl.BlockSpec(gnw.shape, const2)],
        out_specs=[blk, sblk],
        out_shape=[jax.ShapeDtypeStruct(qr.shape, BF16),
                   jax.ShapeDtypeStruct(state.shape[1:], F32)],
        compiler_params=pltpu.CompilerParams(dimension_semantics=("arbitrary",),
                                             vmem_limit_bytes=VMEM_LIMIT),
        name="ret_sample",
    )(qr, kr, vr, gr, state, dec["intra"], dec["qdec"], dec["kdec"], gnw)


def _attn_prompt_kernel(tq, tk, q_ref, k_ref, v_ref, g_ref, o_ref, m_ref, acc_ref):
    qi = pl.program_id(1)
    m_ref[...] = jnp.full(m_ref.shape, NEG_INF, F32)
    acc_ref[...] = jnp.zeros(acc_ref.shape, F32)
    row = qi * tq + lax.broadcasted_iota(jnp.int32, (tq, LANES), 0)
    col = lax.broadcasted_iota(jnp.int32, (tq, LANES), 1)
    ones = jnp.ones((tk, LANES), BF16)
    nslab = tk // LANES

    def score_slabs(hh, start, masked):
        qh = q_ref[:, hh * QHEAD:(hh + 1) * QHEAD]
        kj = k_ref[pl.ds(start, tk), hh * QHEAD:(hh + 1) * QHEAD]
        s = _dot_nt(qh, kj)
        slabs = [s[:, i * LANES:(i + 1) * LANES] for i in range(nslab)]
        if masked:
            slabs = [jnp.where(start + i * LANES + col <= row, sl, NEG_INF)
                     for i, sl in enumerate(slabs)]
        return slabs

    def block(j, masked, st):
        start = pl.multiple_of(j * tk, tk)
        for hh in range(MLA_HEADS):
            slabs = score_slabs(hh, start, masked)
            m_old = m_ref[st, hh]
            m_new = jnp.maximum(m_old, jnp.max(functools.reduce(jnp.maximum, slabs), axis=-1,
                                               keepdims=True))
            alpha = jnp.exp2(m_old - m_new)
            p = jnp.concatenate([jnp.exp2(sl - m_new) for sl in slabs], axis=1).astype(BF16)
            vj = v_ref[pl.ds(start, tk), hh * MLA_DV:(hh + 1) * MLA_DV]
            pv = _dot(p, jnp.concatenate([vj, ones], axis=1))
            acc_ref[st, hh] = acc_ref[st, hh] * jnp.concatenate([alpha, alpha], axis=1) + pv
            m_ref[st, hh] = m_new

    def body(i, carry):
        block(2 * i, False, 0)
        block(2 * i + 1, False, 1)
        return carry

    lax.fori_loop(0, qi // 2, body, 0)

    @pl.when(qi % 2 == 1)
    def _():
        block(qi - 1, False, 0)

    block(qi, True, 1)
    for hh in range(MLA_HEADS):
        m0, m1 = m_ref[0, hh], m_ref[1, hh]
        m = jnp.maximum(m0, m1)
        w0, w1 = jnp.exp2(m0 - m), jnp.exp2(m1 - m)
        acc = (acc_ref[0, hh] * jnp.concatenate([w0, w0], axis=1)
               + acc_ref[1, hh] * jnp.concatenate([w1, w1], axis=1))
        gate = g_ref[:, hh * MLA_DV:(hh + 1) * MLA_DV].astype(F32)
        o_ref[:, hh * MLA_DV:(hh + 1) * MLA_DV] = (acc[:, :MLA_DV] / acc[:, MLA_DV:]
                                                   * gate).astype(o_ref.dtype)


def _attn_prompt(q, kcat, v, ga, *, batch, seq, tq, tk):
    assert tq == tk, "one diagonal key block per query tile"
    nq = seq // tq
    return pl.pallas_call(
        functools.partial(_attn_prompt_kernel, tq, tk),
        grid=(batch, nq),
        in_specs=[pl.BlockSpec((tq, QCOLS), lambda b, i: (b * nq + i, 0)),
                  pl.BlockSpec((seq, QCOLS), lambda b, i: (b, 0)),
                  pl.BlockSpec((seq, MLA_WIDTH), lambda b, i: (b, 0)),
                  pl.BlockSpec((tq, MLA_WIDTH), lambda b, i: (b * nq + i, 0))],
        out_specs=pl.BlockSpec((tq, MLA_WIDTH), lambda b, i: (b * nq + i, 0)),
        out_shape=jax.ShapeDtypeStruct((q.shape[0], MLA_WIDTH), BF16),
        scratch_shapes=[pltpu.VMEM((2, MLA_HEADS, tq, LANES), F32),
                        pltpu.VMEM((2, MLA_HEADS, tq, 2 * MLA_DV), F32)],
        compiler_params=pltpu.CompilerParams(dimension_semantics=("arbitrary", "arbitrary"),
                                             vmem_limit_bytes=VMEM_LIMIT),
        name="attn_prompt",
    )(q, kcat, v, ga)


NBUF = 3


def _col_from_row(row_vec, eye):
    return jnp.sum(jnp.where(eye, jnp.broadcast_to(row_vec, eye.shape), 0.0), axis=1,
                   keepdims=True)


def _page_copies(layer, ppb, pt_ref, ckv_hbm, kpet_hbm, cbuf, kpbuf, sems, step, slot):
    copies = []
    for j in range(ppb):
        page = pt_ref[step * ppb + j]
        rows = pl.ds(j * PAGE_SIZE, PAGE_SIZE)
        copies.append(pltpu.make_async_copy(ckv_hbm.at[layer, page], cbuf.at[slot, rows],
                                            sems.at[slot, 0]))
        copies.append(pltpu.make_async_copy(kpet_hbm.at[layer, page],
                                            kpbuf.at[slot, pl.ds(0, MLA_ROPE), rows],
                                            sems.at[slot, 1]))
    return copies


def _attn_sample_kernel(layer, ppb, t, nblk, pt_ref, q_ref, cnew_ref, kpnew_ref, g_ref, wkbp_ref,
                        wkb_ref, wvb_ref, knw_ref, ones_ref, ckv_hbm, kpet_hbm, o_ref,
                        cbuf, kpbuf, sems, cb_ref, sn_ref, r2_ref, m_ref, l_ref, acc_ref, qtil_ref,
                        b2t_ref):
    step = pl.program_id(0)
    nstep = pl.num_programs(0) - 1
    slot = step % 2
    qbuf = (step // nblk) % 2
    group = MLA_HEADS * t
    nsub = LANES // group
    rows = cbuf.shape[1]
    sub = rows // nsub
    copies = functools.partial(_page_copies, layer, ppb, pt_ref, ckv_hbm, kpet_hbm, cbuf, kpbuf,
                               sems)

    ri = lax.broadcasted_iota(jnp.int32, (LANES, LANES), 0)
    ci = lax.broadcasted_iota(jnp.int32, (LANES, LANES), 1)
    eye = ri == ci

    def build_operands(buf):
        q = q_ref[...]
        qn = jnp.concatenate([q[:, hh * QHEAD:hh * QHEAD + LANES] for hh in range(MLA_HEADS)],
                             axis=1) * jnp.concatenate([knw_ref[...]] * MLA_HEADS, axis=1)
        qn = jnp.concatenate([qn] * (LANES // t), axis=0)
        rh = (lax.broadcasted_iota(jnp.int32, qn.shape, 0) % group) // t
        lh = lax.broadcasted_iota(jnp.int32, qn.shape, 1) // MLA_NOPE
        qt = jnp.where(rh == lh, qn, 0.0).astype(BF16)
        qtil = _dot_nt(wkb_ref[...], qt)
        lane_g = lax.broadcasted_iota(jnp.int32, qtil.shape, 1) // group
        qp = jnp.concatenate([q[:, hh * QHEAD + LANES:(hh + 1) * QHEAD] for hh in range(MLA_HEADS)],
                             axis=0)
        qp = jnp.concatenate([qp] * nsub, axis=0)
        row_g = lax.broadcasted_iota(jnp.int32, qp.shape, 0) // group
        for g in range(nsub):
            qtil_ref[buf, g] = jnp.where(lane_g == g, qtil, 0.0).astype(BF16)
            b2t_ref[buf, g, LANES:, :] = jnp.concatenate(
                [jnp.zeros(qp.shape, F32), jnp.where(row_g == g, qp, 0.0)], axis=1).astype(BF16)

    @pl.when(step == 0)
    def _():
        kpbuf[:, MLA_ROPE:, :] = jnp.zeros((NBUF, LANES - MLA_ROPE, rows), F32)
        for buf in range(2):
            for g in range(nsub):
                b2t_ref[buf, g, :LANES, :] = ones_ref[g]
        sn_ref[1] = jnp.zeros(sn_ref.shape[1:], F32)
        r2_ref[1] = jnp.zeros(r2_ref.shape[1:], F32)
        cb_ref[1] = jnp.zeros(cb_ref.shape[1:], BF16)
        m_ref[...] = jnp.full(m_ref.shape, NEG_INF, F32)
        l_ref[...] = jnp.zeros(l_ref.shape, F32)
        acc_ref[...] = jnp.zeros(acc_ref.shape, F32)
        build_operands(0)
        for blk in range(NBUF - 1):
            for cp in copies(blk, blk):
                cp.start()

    def scores(c_bf, kp_pad, buf, g):
        kf = _dot(c_bf, wkbp_ref[...])
        sq = kf * kf
        fold = (sq[:, 0:LANES] + sq[:, LANES:2 * LANES]) + (sq[:, 2 * LANES:3 * LANES]
                                                            + sq[:, 3 * LANES:4 * LANES])
        lhs2 = jnp.concatenate([fold.astype(BF16), kp_pad.astype(BF16)], axis=1)
        return _dot(c_bf, qtil_ref[buf, g]), _dot_nt(lhs2, b2t_ref[buf, g])

    def softmax_step(sn, r2, valid):
        s = sn * lax.rsqrt(r2[:, :LANES] * (1.0 / MLA_NOPE) + EPS) + r2[:, LANES:]
        if valid is not None:
            s = jnp.where(valid, s, NEG_INF)
        m_old = m_ref[0:1, :]
        m_new = jnp.maximum(m_old, jnp.max(s, axis=0, keepdims=True))
        alpha = jnp.exp(m_old - m_new)
        p = jnp.exp(s - m_new)
        if valid is not None:
            p = jnp.where(valid, p, 0.0)
        l_ref[0:1, :] = l_ref[0:1, :] * alpha + jnp.sum(p, axis=0, keepdims=True)
        m_ref[0:1, :] = m_new
        return p, _col_from_row(alpha, eye)

    ring = step % NBUF
    ahead = jnp.minimum(step + NBUF - 1, nstep - 1)
    ring_ahead = (step + NBUF - 1) % NBUF

    def stages(cur):
        prv = 1 - cur
        for cp in copies(jnp.minimum(step, nstep - 1), ring):
            cp.wait()
        for cp in copies(ahead, ring_ahead):
            cp.start()

        p, alpha_col = softmax_step(sn_ref[prv], r2_ref[prv], None)
        lane_g = lax.broadcasted_iota(jnp.int32, p.shape, 1) // group
        p_rows = jnp.concatenate([jnp.where(lane_g == g, p, 0.0) for g in range(nsub)], axis=0)
        acc_ref[...] = acc_ref[...] * alpha_col + _dot_tn(p_rows.astype(BF16), cb_ref[prv])

        sn = r2 = None
        for g in range(nsub):
            rs = pl.ds(g * sub, sub)
            c_g = cbuf[ring, rs, :].astype(BF16)
            cb_ref[cur, rs, :] = c_g
            sn_g, r2_g = scores(c_g, kpbuf[ring, :, rs].T, qbuf, g)
            sn = sn_g if sn is None else sn + sn_g
            r2 = r2_g if r2 is None else r2 + r2_g
        sn_ref[cur] = sn
        r2_ref[cur] = r2

    for par in range(2):
        pl.when(slot == par)(functools.partial(stages, par))

    @pl.when(step % nblk == 0)
    def _():
        @pl.when(step > 0)
        def _():
            m_col = _col_from_row(m_ref[0:1, :], eye)
            l_col = _col_from_row(l_ref[0:1, :], eye)
            parts = [slice(g * group, (g + 1) * group) for g in range(nsub)]
            m_fin = functools.reduce(jnp.maximum, [m_col[sl] for sl in parts])
            w = [jnp.exp(m_col[sl] - m_fin) for sl in parts]
            l_fin = sum(w[g] * l_col[parts[g]] for g in range(nsub))
            lat = sum(w[g] * acc_ref[parts[g], :] for g in range(nsub)) / l_fin
            full = _dot(lat.astype(BF16), wvb_ref[...])
            o = jnp.concatenate([full[hh * t:(hh + 1) * t, hh * MLA_DV:(hh + 1) * MLA_DV]
                                 for hh in range(MLA_HEADS)], axis=1)
            o_ref[...] = (o * g_ref[...]).astype(o_ref.dtype)

        @pl.when(step < nstep)
        def _():
            m_ref[...] = jnp.full(m_ref.shape, NEG_INF, F32)
            l_ref[...] = jnp.zeros(l_ref.shape, F32)
            pad = 2 * SUBLANES - t
            c_new = jnp.concatenate([cnew_ref[...], jnp.zeros((pad, KV_RANK), F32)],
                                    axis=0).astype(BF16)
            kp_new = jnp.concatenate([kpnew_ref[...], jnp.zeros((pad, LANES), F32)], axis=0)
            r = lax.broadcasted_iota(jnp.int32, (2 * SUBLANES, LANES), 0)
            j = lax.broadcasted_iota(jnp.int32, (2 * SUBLANES, LANES), 1)
            sn0, r20 = scores(c_new, kp_new, qbuf, 0)
            p0, _ = softmax_step(sn0, r20, (r < t) & (r <= j % t) & (j < group))
            acc_ref[...] = _dot_tn(p0.astype(BF16), c_new)

    @pl.when(jnp.logical_and((step + 1) % nblk == 0, step + 1 < nstep))
    def _():
        build_operands(1 - qbuf)

    @pl.when(step == nstep)
    def _():
        for k in range(1, NBUF):
            for cp in copies(nstep - 1, (step + k) % NBUF):
                cp.wait()


def _attn_sample(page_table, q, c_new, kp_new, ga, w_kb, w_vb, kn_w, cache_ckv, cache_kpet, *,
                 layer, t, ppb):
    nseq, npages = page_table.shape
    nblk = npages // ppb
    rows = ppb * PAGE_SIZE
    group = MLA_HEADS * t
    nsub = LANES // group
    fold = MLA_NOPE // nsub
    w_kbp = w_kb.reshape(KV_RANK, MLA_HEADS, nsub, fold).transpose(0, 2, 1, 3).reshape(w_kb.shape)
    j = np.arange(LANES)
    d = np.arange(2 * LANES)
    ones_top = ((d[None, None, :] < LANES)
                & (d[None, None, :] // fold == ((j % group) // t)[None, :, None])
                & (j[None, :, None] // group == np.arange(nsub)[:, None, None]))
    ones_top = jnp.asarray(ones_top, BF16)
    assert nblk >= 2, "the projection / softmax pipeline needs at least two blocks per sequence"
    nstep = nseq * nblk
    seq_next = lambda s, pt: (jnp.minimum((s + 1) // nblk, nseq - 1), 0)
    seq_now = lambda s, pt: (jnp.minimum(s // nblk, nseq - 1), 0)
    seq_prev = lambda s, pt: (jnp.maximum(s - 1, 0) // nblk, 0)
    const = lambda s, pt: (0, 0)
    sub = rows // nsub
    grid_spec = pltpu.PrefetchScalarGridSpec(
        num_scalar_prefetch=1,
        grid=(nstep + 1,),
        in_specs=[pl.BlockSpec((t, QCOLS), seq_next),
                  pl.BlockSpec((t, KV_RANK), seq_now),
                  pl.BlockSpec((t, LANES), seq_now),
                  pl.BlockSpec((t, MLA_WIDTH), seq_prev),
                  pl.BlockSpec(w_kbp.shape, const),
                  pl.BlockSpec(w_kb.shape, const),
                  pl.BlockSpec(w_vb.shape, const),
                  pl.BlockSpec(kn_w.shape, const),
                  pl.BlockSpec(ones_top.shape, lambda s, pt: (0, 0, 0)),
                  pl.BlockSpec(memory_space=pl.ANY),
                  pl.BlockSpec(memory_space=pl.ANY)],
        out_specs=pl.BlockSpec((t, MLA_WIDTH), seq_prev),
        scratch_shapes=[pltpu.VMEM((NBUF, rows, KV_RANK), F32),
                        pltpu.VMEM((NBUF, LANES, rows), F32),
                        pltpu.SemaphoreType.DMA((NBUF, 2)),
                        pltpu.VMEM((2, rows, KV_RANK), BF16),
                        pltpu.VMEM((2, sub, LANES), F32),
                        pltpu.VMEM((2, sub, 2 * LANES), F32),
                        pltpu.VMEM((SUBLANES, LANES), F32),
                        pltpu.VMEM((SUBLANES, LANES), F32),
                        pltpu.VMEM((LANES, KV_RANK), F32),
                        pltpu.VMEM((2, nsub, KV_RANK, LANES), BF16),
                        pltpu.VMEM((2, nsub, 2 * LANES, 2 * LANES), BF16)])
    return pl.pallas_call(
        functools.partial(_attn_sample_kernel, layer, ppb, t, nblk),
        grid_spec=grid_spec,
        out_shape=jax.ShapeDtypeStruct((nseq * t, MLA_WIDTH), F32),
        compiler_params=pltpu.CompilerParams(dimension_semantics=("arbitrary",),
                                             vmem_limit_bytes=VMEM_LIMIT),
        name="attn_sample",
    )(page_table.reshape(-1), q, c_new, kp_new, ga, w_kbp, w_kb, w_vb, kn_w, ones_top, cache_ckv,
      cache_kpet)


def _outproj_kernel(or_ref, oa_ref, x_ref, w_ref, y_ref):
    y = _dot(or_ref[...].astype(BF16), w_ref[:RET_WIDTH, :])
    y = y + _dot(oa_ref[...].astype(BF16), w_ref[RET_WIDTH:, :])
    y_ref[...] = x_ref[...] + y


def _outproj(o_r, o_a, x, w_out, *, tm):
    n = x.shape[0]
    row = lambda i: (i, 0)
    return pl.pallas_call(
        _outproj_kernel,
        grid=(n // tm,),
        in_specs=[pl.BlockSpec((tm, RET_WIDTH), row), pl.BlockSpec((tm, MLA_WIDTH), row),
                  pl.BlockSpec((tm, D_MODEL), row), pl.BlockSpec(w_out.shape, lambda i: (0, 0))],
        out_specs=pl.BlockSpec((tm, D_MODEL), row),
        out_shape=jax.ShapeDtypeStruct(x.shape, F32),
        compiler_params=pltpu.CompilerParams(dimension_semantics=("arbitrary",),
                                             vmem_limit_bytes=VMEM_LIMIT),
        name="outproj",
    )(o_r, o_a, x, w_out)


def _rope_tables(pos):
    def cs(dim):
        inv = ROPE_BASE ** (-(jnp.arange(0, dim, 2, dtype=F32) / dim))
        ang = pos.astype(F32)[:, None] * inv[None, :]
        return jnp.cos(ang), jnp.sin(ang)

    cr, sr = cs(RET_DK)
    ca, sa = cs(MLA_ROPE)
    zeros = jnp.zeros((pos.shape[0], LANES - MLA_ROPE), F32)
    return (jnp.concatenate([cr, cr], axis=1), jnp.concatenate([-sr, sr], axis=1),
            jnp.concatenate([ca, ca, zeros], axis=1), jnp.concatenate([-sa, sa, zeros], axis=1))


def _decay_tables(chunk, nseq):
    log_g = jnp.log1p(-jnp.exp2(-5.0 - jnp.arange(RET_HEADS, dtype=F32)))
    i = jnp.arange(chunk, dtype=F32)
    diff = i[:, None] - i[None, :]
    intra = jnp.where(diff >= 0, jnp.exp(log_g[:, None, None] * jnp.maximum(diff, 0.0)), 0.0)
    if nseq > 1:
        same = jnp.eye(nseq, dtype=F32)
        intra = jnp.einsum("ab,hij->haibj", same, intra).reshape(RET_HEADS, nseq * chunk,
                                                                 nseq * chunk)
    q_dec = jnp.exp(log_g[None, :] * (i[:, None] + 1.0))
    k_dec = jnp.exp(log_g[None, :] * (chunk - 1.0 - i[:, None]))
    widen = lambda a: jnp.tile(jnp.repeat(a, RET_DK, axis=1), (nseq, 1))
    g = 1.0 - np.exp2(-5.0 - np.arange(RET_HEADS, dtype=np.float64))
    sdec = tuple(float(v) for v in np.exp(np.log(g) * chunk))
    return {"intra": intra, "qdec": widen(q_dec), "kdec": widen(k_dec), "sdec": sdec}


def _prep_weights(norm_w, w_in, q_a_norm_w, w_qb, qn_w, qp_w, kv_a_norm_w, kp_w, w_kb, kn_w, w_vb,
                  ret_gn_w, w_out):
    depth = w_in.shape[0]
    s = np.cumsum((512, 512, 512, 512, Q_RANK, KV_RANK, MLA_ROPE, MLA_WIDTH))
    w_in_r = jnp.concatenate(
        [w_in[:, :, :s[3]], w_in[:, :, s[6]:s[7]], w_in[:, :, s[3]:s[6]],
         jnp.zeros((depth, D_MODEL, LANES - MLA_ROPE), w_in.dtype)], axis=2).astype(BF16)
    wq = w_qb.reshape(depth, Q_RANK, MLA_HEADS, MLA_NOPE + MLA_ROPE)
    wq = jnp.pad(wq, ((0, 0), (0, 0), (0, 0), (0, QHEAD - MLA_NOPE - MLA_ROPE)))
    wq = wq.reshape(depth, Q_RANK, QCOLS).astype(BF16)
    pad_rope = lambda w: jnp.pad(w, ((0, 0), (0, LANES - MLA_ROPE)))
    layers = []
    for l in range(depth):
        layers.append({
            "norm_w": norm_w[l][None], "w_in": w_in_r[l], "q_a_norm_w": q_a_norm_w[l][None],
            "w_qb": wq[l], "qn_w": qn_w[l][None], "qp_w": pad_rope(qp_w)[l][None],
            "kv_a_norm_w": kv_a_norm_w[l][None], "kp_w": pad_rope(kp_w)[l][None],
            "w_kb": w_kb[l].astype(BF16), "kn_w": kn_w[l][None], "w_vb": w_vb[l].astype(BF16),
            "ret_gn_w": ret_gn_w[l][None], "w_out": w_out[l].astype(BF16)})
    return layers


def kernel(x_prompt, x_sample, cache_ckv, cache_kpe, state_ret, page_table, norm_w, w_in, q_a_norm_w,
           w_qb, qn_w, qp_w, kv_a_norm_w, kp_w, w_kb, kn_w, w_vb, ret_gn_w, w_out):
    b_p, t_p, _ = x_prompt.shape
    b_s, t_s, _ = x_sample.shape
    depth = w_in.shape[0]
    past = page_table.shape[1] * PAGE_SIZE
    tm_p = 512
    tm_s = 512
    ret_nseq = 8
    ret_bb = 4
    ppb = 16
    cache_kpet = jnp.swapaxes(cache_kpe, 2, 3)

    layers = _prep_weights(norm_w, w_in, q_a_norm_w, w_qb, qn_w, qp_w, kv_a_norm_w, kp_w, w_kb, kn_w,
                           w_vb, ret_gn_w, w_out)
    tabs_p = _rope_tables(jnp.arange(t_p, dtype=jnp.int32))
    tabs_s = _rope_tables(past + jnp.arange(t_s, dtype=jnp.int32))
    tabs_s = tuple(jnp.tile(a, (tm_s // t_s, 1)) for a in tabs_s)
    dec_p = _decay_tables(RET_CHUNK, 1)
    dec_s = _decay_tables(t_s, ret_nseq)

    hp = x_prompt.reshape(b_p * t_p, D_MODEL)
    hs = x_sample.reshape(b_s * t_s, D_MODEL)
    outs = {k: [] for k in ("ckv_p", "kpe_p", "ret_p", "ckv_s", "kpe_s", "ret_s")}
    for l in range(depth):
        lw = layers[l]
        qr, kr, vr, gr, ga, q, c, kpe, kcat, v = _front(hp, tabs_p, lw, tm=tm_p, emit_kv=True,
                                                        act_dtype=BF16)
        o_r, s_new = _ret_prompt(qr, kr, vr, gr, dec_p, lw["ret_gn_w"], batch=b_p, seq=t_p,
                                 bb=ret_bb)
        o_a = _attn_prompt(q, kcat, v, ga, batch=b_p, seq=t_p, tq=256, tk=256)
        hp = _outproj(o_r, o_a, hp, lw["w_out"], tm=tm_p)
        outs["ckv_p"].append(c.reshape(b_p, t_p, KV_RANK))
        outs["kpe_p"].append(kpe.reshape(b_p, t_p, MLA_ROPE))
        outs["ret_p"].append(s_new)

        qr, kr, vr, gr, ga, q, c, kpe, kp128 = _front(hs, tabs_s, lw, tm=tm_s, emit_kv=False,
                                                      act_dtype=F32)
        o_r, s_new = _ret_sample(qr, kr, vr, gr, state_ret, dec_s, lw["ret_gn_w"], layer=l, t=t_s,
                                 nseq=ret_nseq)
        o_a = _attn_sample(page_table, q, c, kp128, ga, lw["w_kb"], lw["w_vb"], lw["kn_w"],
                           cache_ckv, cache_kpet, layer=l, t=t_s, ppb=ppb)
        hs = _outproj(o_r, o_a, hs, lw["w_out"], tm=tm_s)
        outs["ckv_s"].append(c.reshape(b_s, t_s, KV_RANK))
        outs["kpe_s"].append(kpe.reshape(b_s, t_s, MLA_ROPE))
        outs["ret_s"].append(s_new)

    return (hp.reshape(b_p, t_p, D_MODEL), hs.reshape(b_s, t_s, D_MODEL),
            jnp.stack(outs["ckv_p"]), jnp.stack(outs["kpe_p"]), jnp.stack(outs["ret_p"]),
            jnp.stack(outs["ckv_s"]), jnp.stack(outs["kpe_s"]), jnp.stack(outs["ret_s"]))
```

```python
import functools

import numpy as np
import jax
import jax.numpy as jnp
from jax import lax
from jax.experimental import pallas as pl
from jax.experimental.pallas import tpu as pltpu

F32 = jnp.float32
BF16 = jnp.bfloat16

D_MODEL = 1024
PAGE_SIZE = 128
RET_HEADS = 4
RET_DK = 128
RET_DV = 128
RET_WIDTH = RET_HEADS * RET_DV
RET_CHUNK = 128
MLA_HEADS = 4
MLA_NOPE = 128
MLA_ROPE = 64
MLA_DV = 128
MLA_WIDTH = MLA_HEADS * MLA_DV
Q_RANK = 384
KV_RANK = 256
MLA_SCALE = (MLA_NOPE + MLA_ROPE) ** -0.5
LOG2E = 1.4426950408889634
ROPE_BASE = 10000.0
EPS = 1e-6
GN_EPS = 1e-5
NEG_INF = -1e30

LANES = 128
SUBLANES = 8
VMEM_LIMIT = 56 * 1024 * 1024

OFF_QR = 0
OFF_KR = 512
OFF_VR = 1024
OFF_GR = 1536
OFF_GA = 2048
OFF_QL = 2560
OFF_C = OFF_QL + Q_RANK
OFF_KPE = OFF_C + KV_RANK
IN_COLS_PAD = OFF_KPE + LANES
QHEAD = 2 * LANES
QCOLS = MLA_HEADS * QHEAD

NT_DIMS = (((1,), (1,)), ((), ()))
TN_DIMS = (((0,), (0,)), ((), ()))


def _dot(a, b):
    return jnp.dot(a, b, preferred_element_type=F32)


def _dot_nt(a, b):
    return lax.dot_general(a, b, NT_DIMS, preferred_element_type=F32)


def _dot_tn(a, b):
    return lax.dot_general(a, b, TN_DIMS, preferred_element_type=F32)


def _rms_scale(v, n):
    return lax.rsqrt(jnp.sum(v * v, axis=-1, keepdims=True) * (1.0 / n) + EPS)


def _rope64(u, cos, sin, lane):
    swapped = jnp.where(lane < MLA_ROPE // 2,
                        pltpu.roll(u, LANES - MLA_ROPE // 2, 1),
                        pltpu.roll(u, MLA_ROPE // 2, 1))
    return u * cos + swapped * sin


def _front_kernel(emit_kv, qscale, x_ref, cosr_ref, sinr_ref, cosa_ref, sina_ref, normw_ref, win_ref,
                  qanw_ref, wqb_ref, qnw_ref, qpw_ref, kvnw_ref, kpw_ref, wkb_ref, knw_ref,
                  wvb_ref, qr_ref, kr_ref, vr_ref, gr_ref, ga_ref, q_ref, c_ref, kpe_ref,
                  *kv_refs):
    x = x_ref[...]
    h = (x * _rms_scale(x, D_MODEL) * normw_ref[...]).astype(BF16)
    cosr, sinr = cosr_ref[...], sinr_ref[...]
    cosa, sina = cosa_ref[...], sina_ref[...]
    lane = lax.broadcasted_iota(jnp.int32, cosa.shape, 1)

    half = 2 * RET_DK
    for c0 in range(0, RET_WIDTH, half):
        cs = slice(c0, c0 + half)
        zq = _dot(h, win_ref[:, OFF_QR + c0:OFF_QR + c0 + half])
        zk = _dot(h, win_ref[:, OFF_KR + c0:OFF_KR + c0 + half])
        for d0 in range(0, half, RET_DK):
            sl = slice(c0 + d0, c0 + d0 + RET_DK)
            qh, kh = zq[:, d0:d0 + RET_DK], zk[:, d0:d0 + RET_DK]
            qr_ref[:, sl] = (qh * cosr + pltpu.roll(qh, RET_DK // 2, 1) * sinr).astype(qr_ref.dtype)
            kr_ref[:, sl] = ((kh * cosr + pltpu.roll(kh, RET_DK // 2, 1) * sinr)
                             * (RET_DK ** -0.5)).astype(kr_ref.dtype)
        vr_ref[:, cs] = _dot(h, win_ref[:, OFF_VR + c0:OFF_VR + c0 + half]).astype(vr_ref.dtype)
        zg = _dot(h, win_ref[:, OFF_GR + c0:OFF_GR + c0 + half])
        gr_ref[:, cs] = (zg * jax.nn.sigmoid(zg)).astype(gr_ref.dtype)
        zg = _dot(h, win_ref[:, OFF_GA + c0:OFF_GA + c0 + half])
        ga_ref[:, cs] = (zg * jax.nn.sigmoid(zg)).astype(ga_ref.dtype)

    zql = _dot(h, win_ref[:, OFF_QL:OFF_QL + Q_RANK])
    qa = (zql * _rms_scale(zql, Q_RANK) * qanw_ref[...]).astype(BF16)
    q = _dot(qa, wqb_ref[...])
    for hh in range(MLA_HEADS):
        base = hh * QHEAD
        qn = q[:, base:base + LANES]
        qn = qn * _rms_scale(qn, MLA_NOPE) * qnw_ref[...] * qscale
        qp = q[:, base + LANES:base + QHEAD]
        u = qp * _rms_scale(qp, MLA_ROPE) * qpw_ref[...]
        qp = _rope64(u, cosa, sina, lane) * qscale
        q_ref[:, base:base + LANES] = qn.astype(q_ref.dtype)
        q_ref[:, base + LANES:base + QHEAD] = qp.astype(q_ref.dtype)

    zck = _dot(h, win_ref[:, OFF_C:OFF_C + KV_RANK + LANES])
    zc = zck[:, :KV_RANK]
    c = zc * _rms_scale(zc, KV_RANK) * kvnw_ref[...]
    c_ref[...] = c
    zp = zck[:, KV_RANK:]
    u = zp * _rms_scale(zp, MLA_ROPE) * kpw_ref[...]
    kp = _rope64(u, cosa, sina, lane)
    kpe_ref[...] = kp[:, :MLA_ROPE]

    if emit_kv:
        kcat_ref, v_ref = kv_refs
        cb = c.astype(BF16)
        kf = _dot(cb, wkb_ref[...])
        for hh in range(MLA_HEADS):
            kn = kf[:, hh * MLA_NOPE:(hh + 1) * MLA_NOPE]
            kn = kn * _rms_scale(kn, MLA_NOPE) * knw_ref[...]
            kcat_ref[:, hh * QHEAD:hh * QHEAD + LANES] = kn.astype(kcat_ref.dtype)
            kcat_ref[:, hh * QHEAD + LANES:(hh + 1) * QHEAD] = kp.astype(kcat_ref.dtype)
        v_ref[...] = _dot(cb, wvb_ref[...]).astype(v_ref.dtype)
    else:
        (kp128_ref,) = kv_refs
        kp128_ref[...] = kp


def _front(x, tabs, lw, *, tm, emit_kv, act_dtype):
    n = x.shape[0]
    tper = tabs[0].shape[0]
    nper = tper // tm
    row = lambda i: (i, 0)
    const = lambda i: (0, 0)
    tab_spec = pl.BlockSpec((tm, LANES), lambda i: (i % nper, 0))

    def full(a):
        return pl.BlockSpec(a.shape, const)

    weights = [lw["norm_w"], lw["w_in"], lw["q_a_norm_w"], lw["w_qb"], lw["qn_w"], lw["qp_w"],
               lw["kv_a_norm_w"], lw["kp_w"], lw["w_kb"], lw["kn_w"], lw["w_vb"]]
    in_specs = [pl.BlockSpec((tm, D_MODEL), row)] + [tab_spec] * 4 + [full(w) for w in weights]
    out_shape = [jax.ShapeDtypeStruct((n, 512), BF16)] * 3
    out_shape += [jax.ShapeDtypeStruct((n, 512), BF16)]
    out_shape += [jax.ShapeDtypeStruct((n, 512), act_dtype)]
    out_shape += [jax.ShapeDtypeStruct((n, QCOLS), act_dtype)]
    out_shape += [jax.ShapeDtypeStruct((n, KV_RANK), F32)]
    out_shape += [jax.ShapeDtypeStruct((n, MLA_ROPE), F32)]
    if emit_kv:
        out_shape += [jax.ShapeDtypeStruct((n, QCOLS), BF16)]
        out_shape += [jax.ShapeDtypeStruct((n, MLA_WIDTH), BF16)]
    else:
        out_shape += [jax.ShapeDtypeStruct((n, LANES), F32)]
    out_specs = [pl.BlockSpec((tm, s.shape[1]), row) for s in out_shape]
    return pl.pallas_call(
        functools.partial(_front_kernel, emit_kv, MLA_SCALE * LOG2E if emit_kv else MLA_SCALE),
        grid=(n // tm,),
        in_specs=in_specs,
        out_specs=out_specs,
        out_shape=out_shape,
        compiler_params=pltpu.CompilerParams(dimension_semantics=("arbitrary",),
                                             vmem_limit_bytes=VMEM_LIMIT),
        name="front_kv" if emit_kv else "front",
    )(x, *tabs, *weights)


def _ret_prompt_kernel(sdec, q_ref, k_ref, v_ref, g_ref, intra_ref, qdec_ref, kdec_ref, gnw_ref,
                       o_ref, sout_ref, s_ref):
    c = pl.program_id(1)

    @pl.when(c == 0)
    def _():
        s_ref[...] = jnp.zeros_like(s_ref)

    for b in range(q_ref.shape[0]):
        for hh in range(RET_HEADS):
            sl = slice(hh * RET_DK, (hh + 1) * RET_DK)
            q, k, v = q_ref[b, :, sl], k_ref[b, :, sl], v_ref[b, :, sl]
            s = s_ref[b, hh]
            a = _dot_nt(q, k) * intra_ref[hh]
            o = _dot(a.astype(BF16), v) + _dot(q, s.astype(BF16)) * qdec_ref[:, sl]
            kd = (k.astype(F32) * kdec_ref[:, sl]).astype(BF16)
            s_ref[b, hh] = s * sdec[hh] + _dot_tn(kd, v)
            mu = jnp.mean(o, axis=-1, keepdims=True)
            d = o - mu
            var = jnp.mean(d * d, axis=-1, keepdims=True)
            on = d * lax.rsqrt(var + GN_EPS) * gnw_ref[:, sl]
            o_ref[b, :, sl] = (on * g_ref[b, :, sl].astype(F32)).astype(o_ref.dtype)

    @pl.when(c == pl.num_programs(1) - 1)
    def _():
        sout_ref[...] = s_ref[...]


def _ret_prompt(qr, kr, vr, gr, dec, gnw, *, batch, seq, bb):
    nc = seq // RET_CHUNK
    as3d = lambda a: a.reshape(batch, seq, RET_WIDTH)
    blk = pl.BlockSpec((bb, RET_CHUNK, RET_WIDTH), lambda b, c: (b, c, 0))
    const2 = lambda b, c: (0, 0)
    o, s = pl.pallas_call(
        functools.partial(_ret_prompt_kernel, dec["sdec"]),
        grid=(batch // bb, nc),
        in_specs=[blk, blk, blk, blk,
                  pl.BlockSpec(dec["intra"].shape, lambda b, c: (0, 0, 0)),
                  pl.BlockSpec(dec["qdec"].shape, const2),
                  pl.BlockSpec(dec["kdec"].shape, const2),
                  pl.BlockSpec(gnw.shape, const2)],
        out_specs=[blk, pl.BlockSpec((bb, RET_HEADS, RET_DK, RET_DV), lambda b, c: (b, 0, 0, 0))],
        out_shape=[jax.ShapeDtypeStruct((batch, seq, RET_WIDTH), BF16),
                   jax.ShapeDtypeStruct((batch, RET_HEADS, RET_DK, RET_DV), F32)],
        scratch_shapes=[pltpu.VMEM((bb, RET_HEADS, RET_DK, RET_DV), F32)],
        compiler_params=pltpu.CompilerParams(dimension_semantics=("arbitrary", "arbitrary"),
                                             vmem_limit_bytes=VMEM_LIMIT),
        name="ret_prompt",
    )(as3d(qr), as3d(kr), as3d(vr), as3d(gr), dec["intra"], dec["qdec"], dec["kdec"], gnw)
    return o.reshape(batch * seq, RET_WIDTH), s


def _ret_sample_kernel(sdec, t, q_ref, k_ref, v_ref, g_ref, st_ref, intra_ref, qdec_ref, kdec_ref,
                       gnw_ref, o_ref, sout_ref):
    nseq = st_ref.shape[0]
    for hh in range(RET_HEADS):
        sl = slice(hh * RET_DK, (hh + 1) * RET_DK)
        q, k, v = q_ref[:, sl], k_ref[:, sl], v_ref[:, sl]
        a = _dot_nt(q, k) * intra_ref[hh]
        o_intra = _dot(a.astype(BF16), v)
        q32 = q.astype(F32)
        v32 = v.astype(F32)
        kd32 = (k.astype(F32) * kdec_ref[:, sl]).astype(BF16).astype(F32)
        qdec = qdec_ref[:, sl]
        gnw = gnw_ref[:, sl]
        for s in range(nseq):
            rows = slice(s * t, (s + 1) * t)
            s0 = st_ref[s, hh]
            o = o_intra[rows] + _dot(q32[rows], s0) * qdec[rows]
            sout_ref[s, hh] = s0 * sdec[hh] + _dot_tn(kd32[rows], v32[rows])
            mu = jnp.mean(o, axis=-1, keepdims=True)
            d = o - mu
            var = jnp.mean(d * d, axis=-1, keepdims=True)
            on = d * lax.rsqrt(var + GN_EPS) * gnw
            o_ref[rows, sl] = (on * g_ref[rows, sl].astype(F32)).astype(o_ref.dtype)


def _ret_sample(qr, kr, vr, gr, state, dec, gnw, *, layer, t, nseq):
    n = qr.shape[0]
    nb = n // t
    rows = nseq * t
    blk = pl.BlockSpec((rows, RET_WIDTH), lambda i: (i, 0))
    sblk = pl.BlockSpec((nseq, RET_HEADS, RET_DK, RET_DV), lambda i: (i, 0, 0, 0))
    sblk_in = pl.BlockSpec((None, nseq, RET_HEADS, RET_DK, RET_DV), lambda i: (layer, i, 0, 0, 0))
    const2 = lambda i: (0, 0)
    return pl.pallas_call(
        functools.partial(_ret_sample_kernel, dec["sdec"], t),
        grid=(nb // nseq,),
        in_specs=[blk, blk, blk, blk, sblk_in,
                  pl.BlockSpec(dec["intra"].shape, lambda i: (0, 0, 0)),
                  pl.BlockSpec(dec["qdec"].shape, const2),
                  pl.BlockSpec(dec["kdec"].shape, const2),
                  pl.BlockSpec(gnw.shape, const2)],
        out_specs=[blk, sblk],
        out_shape=[jax.ShapeDtypeStruct(qr.shape, BF16),
                   jax.ShapeDtypeStruct(state.shape[1:], F32)],
        compiler_params=pltpu.CompilerParams(dimension_semantics=("arbitrary",),
                                             vmem_limit_bytes=VMEM_LIMIT),
        name="ret_sample",
    )(qr, kr, vr, gr, state, dec["intra"], dec["qdec"], dec["kdec"], gnw)


def _attn_prompt_kernel(tq, tk, q_ref, k_ref, v_ref, g_ref, o_ref, m_ref, acc_ref):
    qi = pl.program_id(1)
    m_ref[...] = jnp.full(m_ref.shape, NEG_INF, F32)
    acc_ref[...] = jnp.zeros(acc_ref.shape, F32)
    row = qi * tq + lax.broadcasted_iota(jnp.int32, (tq, LANES), 0)
    col = lax.broadcasted_iota(jnp.int32, (tq, LANES), 1)
    ones = jnp.ones((tk, LANES), BF16)
    nslab = tk // LANES

    def score_slabs(hh, start, masked):
        qh = q_ref[:, hh * QHEAD:(hh + 1) * QHEAD]
        kj = k_ref[pl.ds(start, tk), hh * QHEAD:(hh + 1) * QHEAD]
        s = _dot_nt(qh, kj)
        slabs = [s[:, i * LANES:(i + 1) * LANES] for i in range(nslab)]
        if masked:
            slabs = [jnp.where(start + i * LANES + col <= row, sl, NEG_INF)
                     for i, sl in enumerate(slabs)]
        return slabs

    def block(j, masked, st):
        start = pl.multiple_of(j * tk, tk)
        for hh in range(MLA_HEADS):
            slabs = score_slabs(hh, start, masked)
            m_old = m_ref[st, hh]
            m_new = jnp.maximum(m_old, jnp.max(functools.reduce(jnp.maximum, slabs), axis=-1,
                                               keepdims=True))
            alpha = jnp.exp2(m_old - m_new)
            p = jnp.concatenate([jnp.exp2(sl - m_new) for sl in slabs], axis=1).astype(BF16)
            vj = v_ref[pl.ds(start, tk), hh * MLA_DV:(hh + 1) * MLA_DV]
            pv = _dot(p, jnp.concatenate([vj, ones], axis=1))
            acc_ref[st, hh] = acc_ref[st, hh] * jnp.concatenate([alpha, alpha], axis=1) + pv
            m_ref[st, hh] = m_new

    def body(i, carry):
        block(2 * i, False, 0)
        block(2 * i + 1, False, 1)
        return carry

    lax.fori_loop(0, qi // 2, body, 0)

    @pl.when(qi % 2 == 1)
    def _():
        block(qi - 1, False, 0)

    block(qi, True, 1)
    for hh in range(MLA_HEADS):
        m0, m1 = m_ref[0, hh], m_ref[1, hh]
        m = jnp.maximum(m0, m1)
        w0, w1 = jnp.exp2(m0 - m), jnp.exp2(m1 - m)
        acc = (acc_ref[0, hh] * jnp.concatenate([w0, w0], axis=1)
               + acc_ref[1, hh] * jnp.concatenate([w1, w1], axis=1))
        gate = g_ref[:, hh * MLA_DV:(hh + 1) * MLA_DV].astype(F32)
        o_ref[:, hh * MLA_DV:(hh + 1) * MLA_DV] = (acc[:, :MLA_DV] / acc[:, MLA_DV:]
                                                   * gate).astype(o_ref.dtype)


def _attn_prompt(q, kcat, v, ga, *, batch, seq, tq, tk):
    assert tq == tk, "one diagonal key block per query tile"
    nq = seq // tq
    return pl.pallas_call(
        functools.partial(_attn_prompt_kernel, tq, tk),
        grid=(batch, nq),
        in_specs=[pl.BlockSpec((tq, QCOLS), lambda b, i: (b * nq + i, 0)),
                  pl.BlockSpec((seq, QCOLS), lambda b, i: (b, 0)),
                  pl.BlockSpec((seq, MLA_WIDTH), lambda b, i: (b, 0)),
                  pl.BlockSpec((tq, MLA_WIDTH), lambda b, i: (b * nq + i, 0))],
        out_specs=pl.BlockSpec((tq, MLA_WIDTH), lambda b, i: (b * nq + i, 0)),
        out_shape=jax.ShapeDtypeStruct((q.shape[0], MLA_WIDTH), BF16),
        scratch_shapes=[pltpu.VMEM((2, MLA_HEADS, tq, LANES), F32),
                        pltpu.VMEM((2, MLA_HEADS, tq, 2 * MLA_DV), F32)],
        compiler_params=pltpu.CompilerParams(dimension_semantics=("arbitrary", "arbitrary"),
                                             vmem_limit_bytes=VMEM_LIMIT),
        name="attn_prompt",
    )(q, kcat, v, ga)


NBUF = 3


def _col_from_row(row_vec, eye):
    return jnp.sum(jnp.where(eye, jnp.broadcast_to(row_vec, eye.shape), 0.0), axis=1,
                   keepdims=True)


def _page_copies(layer, ppb, pt_ref, ckv_hbm, kpet_hbm, cbuf, kpbuf, sems, step, slot):
    copies = []
    for j in range(ppb):
        page = pt_ref[step * ppb + j]
        rows = pl.ds(j * PAGE_SIZE, PAGE_SIZE)
        copies.append(pltpu.make_async_copy(ckv_hbm.at[layer, page], cbuf.at[slot, rows],
                                            sems.at[slot, 0]))
        copies.append(pltpu.make_async_copy(kpet_hbm.at[layer, page],
                                            kpbuf.at[slot, pl.ds(0, MLA_ROPE), rows],
                                            sems.at[slot, 1]))
    return copies


def _start_all(copies):
    for i, cp in enumerate(copies):
        cp.start(priority=(i // 2) % 2)


def _attn_sample_kernel(layer, ppb, t, nblk, pt_ref, q_ref, cnew_ref, kpnew_ref, g_ref, wkbp_ref,
                        wkb_ref, wvb_ref, knw_ref, ones_ref, ckv_hbm, kpet_hbm, o_ref,
                        cbuf, kpbuf, sems, cb_ref, sn_ref, r2_ref, m_ref, l_ref, acc_ref, qtil_ref,
                        b2t_ref):
    step = pl.program_id(0)
    nstep = pl.num_programs(0) - 1
    slot = step % 2
    qbuf = (step // nblk) % 2
    group = MLA_HEADS * t
    nsub = LANES // group
    rows = cbuf.shape[1]
    sub = rows // nsub
    copies = functools.partial(_page_copies, layer, ppb, pt_ref, ckv_hbm, kpet_hbm, cbuf, kpbuf,
                               sems)

    ri = lax.broadcasted_iota(jnp.int32, (LANES, LANES), 0)
    ci = lax.broadcasted_iota(jnp.int32, (LANES, LANES), 1)
    eye = ri == ci

    def build_operands(buf):
        q = q_ref[...]
        qn = jnp.concatenate([q[:, hh * QHEAD:hh * QHEAD + LANES] for hh in range(MLA_HEADS)],
                             axis=1) * jnp.concatenate([knw_ref[...]] * MLA_HEADS, axis=1)
        qn = jnp.concatenate([qn] * (LANES // t), axis=0)
        rh = (lax.broadcasted_iota(jnp.int32, qn.shape, 0) % group) // t
        lh = lax.broadcasted_iota(jnp.int32, qn.shape, 1) // MLA_NOPE
        qt = jnp.where(rh == lh, qn, 0.0).astype(BF16)
        qtil = _dot_nt(wkb_ref[...], qt)
        lane_g = lax.broadcasted_iota(jnp.int32, qtil.shape, 1) // group
        qp = jnp.concatenate([q[:, hh * QHEAD + LANES:(hh + 1) * QHEAD] for hh in range(MLA_HEADS)],
                             axis=0)
        qp = jnp.concatenate([qp] * nsub, axis=0)
        row_g = lax.broadcasted_iota(jnp.int32, qp.shape, 0) // group
        for g in range(nsub):
            qtil_ref[buf, g] = jnp.where(lane_g == g, qtil, 0.0).astype(BF16)
            b2t_ref[buf, g, LANES:, :] = jnp.concatenate(
                [jnp.zeros(qp.shape, F32), jnp.where(row_g == g, qp, 0.0)], axis=1).astype(BF16)

    @pl.when(step == 0)
    def _():
        kpbuf[:, MLA_ROPE:, :] = jnp.zeros((NBUF, LANES - MLA_ROPE, rows), F32)
        for buf in range(2):
            for g in range(nsub):
                b2t_ref[buf, g, :LANES, :] = ones_ref[g]
        sn_ref[1] = jnp.zeros(sn_ref.shape[1:], F32)
        r2_ref[1] = jnp.zeros(r2_ref.shape[1:], F32)
        cb_ref[1] = jnp.zeros(cb_ref.shape[1:], BF16)
        m_ref[...] = jnp.full(m_ref.shape, NEG_INF, F32)
        l_ref[...] = jnp.zeros(l_ref.shape, F32)
        acc_ref[...] = jnp.zeros(acc_ref.shape, F32)
        build_operands(0)
        for blk in range(NBUF - 1):
            _start_all(copies(blk, blk))

    def scores(c_bf, kp_pad, buf, g):
        kf = _dot(c_bf, wkbp_ref[...])
        sq = kf * kf
        fold = (sq[:, 0:LANES] + sq[:, LANES:2 * LANES]) + (sq[:, 2 * LANES:3 * LANES]
                                                            + sq[:, 3 * LANES:4 * LANES])
        lhs2 = jnp.concatenate([fold.astype(BF16), kp_pad.astype(BF16)], axis=1)
        return _dot(c_bf, qtil_ref[buf, g]), _dot_nt(lhs2, b2t_ref[buf, g])

    def softmax_step(sn, r2, valid):
        s = sn * lax.rsqrt(r2[:, :LANES] * (1.0 / MLA_NOPE) + EPS) + r2[:, LANES:]
        if valid is not None:
            s = jnp.where(valid, s, NEG_INF)
        m_old = m_ref[0:1, :]
        m_new = jnp.maximum(m_old, jnp.max(s, axis=0, keepdims=True))
        alpha = jnp.exp(m_old - m_new)
        p = jnp.exp(s - m_new)
        if valid is not None:
            p = jnp.where(valid, p, 0.0)
        l_ref[0:1, :] = l_ref[0:1, :] * alpha + jnp.sum(p, axis=0, keepdims=True)
        m_ref[0:1, :] = m_new
        return p, _col_from_row(alpha, eye)

    ring = step % NBUF
    ahead = jnp.minimum(step + NBUF - 1, nstep - 1)
    ring_ahead = (step + NBUF - 1) % NBUF

    def stages(cur):
        prv = 1 - cur
        for cp in copies(jnp.minimum(step, nstep - 1), ring):
            cp.wait()
        _start_all(copies(ahead, ring_ahead))

        p, alpha_col = softmax_step(sn_ref[prv], r2_ref[prv], None)
        lane_g = lax.broadcasted_iota(jnp.int32, p.shape, 1) // group
        p_rows = jnp.concatenate([jnp.where(lane_g == g, p, 0.0) for g in range(nsub)], axis=0)
        acc_ref[...] = acc_ref[...] * alpha_col + _dot_tn(p_rows.astype(BF16), cb_ref[prv])

        sn = r2 = None
        for g in range(nsub):
            rs = pl.ds(g * sub, sub)
            c_g = cbuf[ring, rs, :].astype(BF16)
            cb_ref[cur, rs, :] = c_g
            sn_g, r2_g = scores(c_g, kpbuf[ring, :, rs].T, qbuf, g)
            sn = sn_g if sn is None else sn + sn_g
            r2 = r2_g if r2 is None else r2 + r2_g
        sn_ref[cur] = sn
        r2_ref[cur] = r2

    for par in range(2):
        pl.when(slot == par)(functools.partial(stages, par))

    @pl.when(step % nblk == 0)
    def _():
        @pl.when(step > 0)
        def _():
            m_col = _col_from_row(m_ref[0:1, :], eye)
            l_col = _col_from_row(l_ref[0:1, :], eye)
            parts = [slice(g * group, (g + 1) * group) for g in range(nsub)]
            m_fin = functools.reduce(jnp.maximum, [m_col[sl] for sl in parts])
            w = [jnp.exp(m_col[sl] - m_fin) for sl in parts]
            l_fin = sum(w[g] * l_col[parts[g]] for g in range(nsub))
            lat = sum(w[g] * acc_ref[parts[g], :] for g in range(nsub)) / l_fin
            full = _dot(lat.astype(BF16), wvb_ref[...])
            o = jnp.concatenate([full[hh * t:(hh + 1) * t, hh * MLA_DV:(hh + 1) * MLA_DV]
                                 for hh in range(MLA_HEADS)], axis=1)
            o_ref[...] = (o * g_ref[...]).astype(o_ref.dtype)

        @pl.when(step < nstep)
        def _():
            m_ref[...] = jnp.full(m_ref.shape, NEG_INF, F32)
            l_ref[...] = jnp.zeros(l_ref.shape, F32)
            pad = 2 * SUBLANES - t
            c_new = jnp.concatenate([cnew_ref[...], jnp.zeros((pad, KV_RANK), F32)],
                                    axis=0).astype(BF16)
            kp_new = jnp.concatenate([kpnew_ref[...], jnp.zeros((pad, LANES), F32)], axis=0)
            r = lax.broadcasted_iota(jnp.int32, (2 * SUBLANES, LANES), 0)
            j = lax.broadcasted_iota(jnp.int32, (2 * SUBLANES, LANES), 1)
            sn0, r20 = scores(c_new, kp_new, qbuf, 0)
            p0, _ = softmax_step(sn0, r20, (r < t) & (r <= j % t) & (j < group))
            acc_ref[...] = _dot_tn(p0.astype(BF16), c_new)

    @pl.when(jnp.logical_and((step + 1) % nblk == 0, step + 1 < nstep))
    def _():
        build_operands(1 - qbuf)

    @pl.when(step == nstep)
    def _():
        for k in range(1, NBUF):
            for cp in copies(nstep - 1, (step + k) % NBUF):
                cp.wait()


def _attn_sample(page_table, q, c_new, kp_new, ga, w_kb, w_vb, kn_w, cache_ckv, cache_kpet, *,
                 layer, t, ppb):
    nseq, npages = page_table.shape
    nblk = npages // ppb
    rows = ppb * PAGE_SIZE
    group = MLA_HEADS * t
    nsub = LANES // group
    fold = MLA_NOPE // nsub
    w_kbp = w_kb.reshape(KV_RANK, MLA_HEADS, nsub, fold).transpose(0, 2, 1, 3).reshape(w_kb.shape)
    j = np.arange(LANES)
    d = np.arange(2 * LANES)
    ones_top = ((d[None, None, :] < LANES)
                & (d[None, None, :] // fold == ((j % group) // t)[None, :, None])
                & (j[None, :, None] // group == np.arange(nsub)[:, None, None]))
    ones_top = jnp.asarray(ones_top, BF16)
    assert nblk >= 2, "the projection / softmax pipeline needs at least two blocks per sequence"
    nstep = nseq * nblk
    seq_next = lambda s, pt: (jnp.minimum((s + 1) // nblk, nseq - 1), 0)
    seq_now = lambda s, pt: (jnp.minimum(s // nblk, nseq - 1), 0)
    seq_prev = lambda s, pt: (jnp.maximum(s - 1, 0) // nblk, 0)
    const = lambda s, pt: (0, 0)
    sub = rows // nsub
    grid_spec = pltpu.PrefetchScalarGridSpec(
        num_scalar_prefetch=1,
        grid=(nstep + 1,),
        in_specs=[pl.BlockSpec((t, QCOLS), seq_next),
                  pl.BlockSpec((t, KV_RANK), seq_now),
                  pl.BlockSpec((t, LANES), seq_now),
                  pl.BlockSpec((t, MLA_WIDTH), seq_prev),
                  pl.BlockSpec(w_kbp.shape, const),
                  pl.BlockSpec(w_kb.shape, const),
                  pl.BlockSpec(w_vb.shape, const),
                  pl.BlockSpec(kn_w.shape, const),
                  pl.BlockSpec(ones_top.shape, lambda s, pt: (0, 0, 0)),
                  pl.BlockSpec(memory_space=pl.ANY),
                  pl.BlockSpec(memory_space=pl.ANY)],
        out_specs=pl.BlockSpec((t, MLA_WIDTH), seq_prev),
        scratch_shapes=[pltpu.VMEM((NBUF, rows, KV_RANK), F32),
                        pltpu.VMEM((NBUF, LANES, rows), F32),
                        pltpu.SemaphoreType.DMA((NBUF, 2)),
                        pltpu.VMEM((2, rows, KV_RANK), BF16),
                        pltpu.VMEM((2, sub, LANES), F32),
                        pltpu.VMEM((2, sub, 2 * LANES), F32),
                        pltpu.VMEM((SUBLANES, LANES), F32),
                        pltpu.VMEM((SUBLANES, LANES), F32),
                        pltpu.VMEM((LANES, KV_RANK), F32),
                        pltpu.VMEM((2, nsub, KV_RANK, LANES), BF16),
                        pltpu.VMEM((2, nsub, 2 * LANES, 2 * LANES), BF16)])
    return pl.pallas_call(
        functools.partial(_attn_sample_kernel, layer, ppb, t, nblk),
        grid_spec=grid_spec,
        out_shape=jax.ShapeDtypeStruct((nseq * t, MLA_WIDTH), F32),
        compiler_params=pltpu.CompilerParams(dimension_semantics=("arbitrary",),
                                             vmem_limit_bytes=VMEM_LIMIT),
        name="attn_sample",
    )(page_table.reshape(-1), q, c_new, kp_new, ga, w_kbp, w_kb, w_vb, kn_w, ones_top, cache_ckv,
      cache_kpet)


def _outproj_kernel(or_ref, oa_ref, x_ref, w_ref, y_ref):
    y = _dot(or_ref[...].astype(BF16), w_ref[:RET_WIDTH, :])
    y = y + _dot(oa_ref[...].astype(BF16), w_ref[RET_WIDTH:, :])
    y_ref[...] = x_ref[...] + y


def _outproj(o_r, o_a, x, w_out, *, tm):
    n = x.shape[0]
    row = lambda i: (i, 0)
    return pl.pallas_call(
        _outproj_kernel,
        grid=(n // tm,),
        in_specs=[pl.BlockSpec((tm, RET_WIDTH), row), pl.BlockSpec((tm, MLA_WIDTH), row),
                  pl.BlockSpec((tm, D_MODEL), row), pl.BlockSpec(w_out.shape, lambda i: (0, 0))],
        out_specs=pl.BlockSpec((tm, D_MODEL), row),
        out_shape=jax.ShapeDtypeStruct(x.shape, F32),
        compiler_params=pltpu.CompilerParams(dimension_semantics=("arbitrary",),
                                             vmem_limit_bytes=VMEM_LIMIT),
        name="outproj",
    )(o_r, o_a, x, w_out)


def _rope_tables(pos):
    def cs(dim):
        inv = ROPE_BASE ** (-(jnp.arange(0, dim, 2, dtype=F32) / dim))
        ang = pos.astype(F32)[:, None] * inv[None, :]
        return jnp.cos(ang), jnp.sin(ang)

    cr, sr = cs(RET_DK)
    ca, sa = cs(MLA_ROPE)
    zeros = jnp.zeros((pos.shape[0], LANES - MLA_ROPE), F32)
    return (jnp.concatenate([cr, cr], axis=1), jnp.concatenate([-sr, sr], axis=1),
            jnp.concatenate([ca, ca, zeros], axis=1), jnp.concatenate([-sa, sa, zeros], axis=1))


def _decay_tables(chunk, nseq):
    log_g = jnp.log1p(-jnp.exp2(-5.0 - jnp.arange(RET_HEADS, dtype=F32)))
    i = jnp.arange(chunk, dtype=F32)
    diff = i[:, None] - i[None, :]
    intra = jnp.where(diff >= 0, jnp.exp(log_g[:, None, None] * jnp.maximum(diff, 0.0)), 0.0)
    if nseq > 1:
        same = jnp.eye(nseq, dtype=F32)
        intra = jnp.einsum("ab,hij->haibj", same, intra).reshape(RET_HEADS, nseq * chunk,
                                                                 nseq * chunk)
    q_dec = jnp.exp(log_g[None, :] * (i[:, None] + 1.0))
    k_dec = jnp.exp(log_g[None, :] * (chunk - 1.0 - i[:, None]))
    widen = lambda a: jnp.tile(jnp.repeat(a, RET_DK, axis=1), (nseq, 1))
    g = 1.0 - np.exp2(-5.0 - np.arange(RET_HEADS, dtype=np.float64))
    sdec = tuple(float(v) for v in np.exp(np.log(g) * chunk))
    return {"intra": intra, "qdec": widen(q_dec), "kdec": widen(k_dec), "sdec": sdec}


def _prep_weights(norm_w, w_in, q_a_norm_w, w_qb, qn_w, qp_w, kv_a_norm_w, kp_w, w_kb, kn_w, w_vb,
                  ret_gn_w, w_out):
    depth = w_in.shape[0]
    s = np.cumsum((512, 512, 512, 512, Q_RANK, KV_RANK, MLA_ROPE, MLA_WIDTH))
    w_in_r = jnp.concatenate(
        [w_in[:, :, :s[3]], w_in[:, :, s[6]:s[7]], w_in[:, :, s[3]:s[6]],
         jnp.zeros((depth, D_MODEL, LANES - MLA_ROPE), w_in.dtype)], axis=2).astype(BF16)
    wq = w_qb.reshape(depth, Q_RANK, MLA_HEADS, MLA_NOPE + MLA_ROPE)
    wq = jnp.pad(wq, ((0, 0), (0, 0), (0, 0), (0, QHEAD - MLA_NOPE - MLA_ROPE)))
    wq = wq.reshape(depth, Q_RANK, QCOLS).astype(BF16)
    pad_rope = lambda w: jnp.pad(w, ((0, 0), (0, LANES - MLA_ROPE)))
    layers = []
    for l in range(depth):
        layers.append({
            "norm_w": norm_w[l][None], "w_in": w_in_r[l], "q_a_norm_w": q_a_norm_w[l][None],
            "w_qb": wq[l], "qn_w": qn_w[l][None], "qp_w": pad_rope(qp_w)[l][None],
            "kv_a_norm_w": kv_a_norm_w[l][None], "kp_w": pad_rope(kp_w)[l][None],
            "w_kb": w_kb[l].astype(BF16), "kn_w": kn_w[l][None], "w_vb": w_vb[l].astype(BF16),
            "ret_gn_w": ret_gn_w[l][None], "w_out": w_out[l].astype(BF16)})
    return layers


def kernel(x_prompt, x_sample, cache_ckv, cache_kpe, state_ret, page_table, norm_w, w_in, q_a_norm_w,
           w_qb, qn_w, qp_w, kv_a_norm_w, kp_w, w_kb, kn_w, w_vb, ret_gn_w, w_out):
    b_p, t_p, _ = x_prompt.shape
    b_s, t_s, _ = x_sample.shape
    depth = w_in.shape[0]
    past = page_table.shape[1] * PAGE_SIZE
    tm_p = 512
    tm_s = 512
    ret_nseq = 8
    ret_bb = 4
    ppb = 32
    cache_kpet = jnp.swapaxes(cache_kpe, 2, 3)

    layers = _prep_weights(norm_w, w_in, q_a_norm_w, w_qb, qn_w, qp_w, kv_a_norm_w, kp_w, w_kb, kn_w,
                           w_vb, ret_gn_w, w_out)
    tabs_p = _rope_tables(jnp.arange(t_p, dtype=jnp.int32))
    tabs_s = _rope_tables(past + jnp.arange(t_s, dtype=jnp.int32))
    tabs_s = tuple(jnp.tile(a, (tm_s // t_s, 1)) for a in tabs_s)
    dec_p = _decay_tables(RET_CHUNK, 1)
    dec_s = _decay_tables(t_s, ret_nseq)

    hp = x_prompt.reshape(b_p * t_p, D_MODEL)
    hs = x_sample.reshape(b_s * t_s, D_MODEL)
    outs = {k: [] for k in ("ckv_p", "kpe_p", "ret_p", "ckv_s", "kpe_s", "ret_s")}
    for l in range(depth):
        lw = layers[l]
        qr, kr, vr, gr, ga, q, c, kpe, kcat, v = _front(hp, tabs_p, lw, tm=tm_p, emit_kv=True,
                                                        act_dtype=BF16)
        o_r, s_new = _ret_prompt(qr, kr, vr, gr, dec_p, lw["ret_gn_w"], batch=b_p, seq=t_p,
                                 bb=ret_bb)
        o_a = _attn_prompt(q, kcat, v, ga, batch=b_p, seq=t_p, tq=256, tk=256)
        hp = _outproj(o_r, o_a, hp, lw["w_out"], tm=tm_p)
        outs["ckv_p"].append(c.reshape(b_p, t_p, KV_RANK))
        outs["kpe_p"].append(kpe.reshape(b_p, t_p, MLA_ROPE))
        outs["ret_p"].append(s_new)

        qr, kr, vr, gr, ga, q, c, kpe, kp128 = _front(hs, tabs_s, lw, tm=tm_s, emit_kv=False,
                                                      act_dtype=F32)
        o_r, s_new = _ret_sample(qr, kr, vr, gr, state_ret, dec_s, lw["ret_gn_w"], layer=l, t=t_s,
                                 nseq=ret_nseq)
        o_a = _attn_sample(page_table, q, c, kp128, ga, lw["w_kb"], lw["w_vb"], lw["kn_w"],
                           cache_ckv, cache_kpet, layer=l, t=t_s, ppb=ppb)
        hs = _outproj(o_r, o_a, hs, lw["w_out"], tm=tm_s)
        outs["ckv_s"].append(c.reshape(b_s, t_s, KV_RANK))
        outs["kpe_s"].append(kpe.reshape(b_s, t_s, MLA_ROPE))
        outs["ret_s"].append(s_new)

    return (hp.reshape(b_p, t_p, D_MODEL), hs.reshape(b_s, t_s, D_MODEL),
            jnp.stack(outs["ckv_p"]), jnp.stack(outs["kpe_p"]), jnp.stack(outs["ret_p"]),
            jnp.stack(outs["ckv_s"]), jnp.stack(outs["kpe_s"]), jnp.stack(outs["ret_s"]))
```

```python
import functools

import numpy as np
import jax
import jax.numpy as jnp
from jax import lax
from jax.experimental import pallas as pl
from jax.experimental.pallas import tpu as pltpu

F32 = jnp.float32
BF16 = jnp.bfloat16

D_MODEL = 1024
PAGE_SIZE = 128
RET_HEADS = 4
RET_DK = 128
RET_DV = 128
RET_WIDTH = RET_HEADS * RET_DV
RET_CHUNK = 128
MLA_HEADS = 4
MLA_NOPE = 128
MLA_ROPE = 64
MLA_DV = 128
MLA_WIDTH = MLA_HEADS * MLA_DV
Q_RANK = 384
KV_RANK = 256
MLA_SCALE = (MLA_NOPE + MLA_ROPE) ** -0.5
LOG2E = 1.4426950408889634
ROPE_BASE = 10000.0
EPS = 1e-6
GN_EPS = 1e-5
NEG_INF = -1e30

LANES = 128
SUBLANES = 8
VMEM_LIMIT = 56 * 1024 * 1024

OFF_QR = 0
OFF_KR = 512
OFF_VR = 1024
OFF_GR = 1536
OFF_QL = 2048
OFF_C = OFF_QL + Q_RANK
OFF_KPE = OFF_C + KV_RANK
IN_COLS_PAD = OFF_KPE + LANES
QHEAD = 2 * LANES
QCOLS = MLA_HEADS * QHEAD

NT_DIMS = (((1,), (1,)), ((), ()))
TN_DIMS = (((0,), (0,)), ((), ()))


def _dot(a, b):
    return jnp.dot(a, b, preferred_element_type=F32)


def _dot_nt(a, b):
    return lax.dot_general(a, b, NT_DIMS, preferred_element_type=F32)


def _dot_tn(a, b):
    return lax.dot_general(a, b, TN_DIMS, preferred_element_type=F32)


def _rms_scale(v, n):
    return lax.rsqrt(jnp.sum(v * v, axis=-1, keepdims=True) * (1.0 / n) + EPS)


def _rope64(u, cos, sin, lane):
    swapped = jnp.where(lane < MLA_ROPE // 2,
                        pltpu.roll(u, LANES - MLA_ROPE // 2, 1),
                        pltpu.roll(u, MLA_ROPE // 2, 1))
    return u * cos + swapped * sin


def _front_kernel(emit_kv, qscale, x_ref, cosr_ref, sinr_ref, cosa_ref, sina_ref, normw_ref, win_ref, wga_ref,
                  qanw_ref, wqb_ref, qnw_ref, qpw_ref, kvnw_ref, kpw_ref, wkb_ref, knw_ref,
                  wvb_ref, qr_ref, kr_ref, vr_ref, gr_ref, ga_ref, q_ref, c_ref, kpe_ref,
                  *kv_refs):
    x = x_ref[...]
    h = (x * _rms_scale(x, D_MODEL) * normw_ref[...]).astype(BF16)
    cosr, sinr = cosr_ref[...], sinr_ref[...]
    cosa, sina = cosa_ref[...], sina_ref[...]
    lane = lax.broadcasted_iota(jnp.int32, cosa.shape, 1)

    half = 2 * RET_DK
    for c0 in range(0, RET_WIDTH, half):
        cs = slice(c0, c0 + half)
        zq = _dot(h, win_ref[:, OFF_QR + c0:OFF_QR + c0 + half])
        zk = _dot(h, win_ref[:, OFF_KR + c0:OFF_KR + c0 + half])
        for d0 in range(0, half, RET_DK):
            sl = slice(c0 + d0, c0 + d0 + RET_DK)
            qh, kh = zq[:, d0:d0 + RET_DK], zk[:, d0:d0 + RET_DK]
            qr_ref[:, sl] = (qh * cosr + pltpu.roll(qh, RET_DK // 2, 1) * sinr).astype(qr_ref.dtype)
            kr_ref[:, sl] = ((kh * cosr + pltpu.roll(kh, RET_DK // 2, 1) * sinr)
                             * (RET_DK ** -0.5)).astype(kr_ref.dtype)
        vr_ref[:, cs] = _dot(h, win_ref[:, OFF_VR + c0:OFF_VR + c0 + half]).astype(vr_ref.dtype)
        zg = _dot(h, win_ref[:, OFF_GR + c0:OFF_GR + c0 + half])
        gr_ref[:, cs] = (zg * jax.nn.sigmoid(zg)).astype(gr_ref.dtype)
        zg = _dot(h, wga_ref[:, cs])
        ga_ref[:, cs] = (zg * jax.nn.sigmoid(zg)).astype(ga_ref.dtype)

    zql = _dot(h, win_ref[:, OFF_QL:OFF_QL + Q_RANK])
    qa = (zql * _rms_scale(zql, Q_RANK) * qanw_ref[...]).astype(BF16)
    q = _dot(qa, wqb_ref[...])
    for hh in range(MLA_HEADS):
        base = hh * QHEAD
        qn = q[:, base:base + LANES]
        qn = qn * _rms_scale(qn, MLA_NOPE) * qnw_ref[...] * qscale
        qp = q[:, base + LANES:base + QHEAD]
        u = qp * _rms_scale(qp, MLA_ROPE) * qpw_ref[...]
        qp = _rope64(u, cosa, sina, lane) * qscale
        q_ref[:, base:base + LANES] = qn.astype(q_ref.dtype)
        q_ref[:, base + LANES:base + QHEAD] = qp.astype(q_ref.dtype)

    zck = _dot(h, win_ref[:, OFF_C:OFF_C + KV_RANK + LANES])
    zc = zck[:, :KV_RANK]
    c = zc * _rms_scale(zc, KV_RANK) * kvnw_ref[...]
    c_ref[...] = c
    zp = zck[:, KV_RANK:]
    u = zp * _rms_scale(zp, MLA_ROPE) * kpw_ref[...]
    kp = _rope64(u, cosa, sina, lane)
    kpe_ref[...] = kp[:, :MLA_ROPE]

    if emit_kv:
        kcat_ref, v_ref = kv_refs
        cb = c.astype(BF16)
        kf = _dot(cb, wkb_ref[...])
        for hh in range(MLA_HEADS):
            kn = kf[:, hh * MLA_NOPE:(hh + 1) * MLA_NOPE]
            kn = kn * _rms_scale(kn, MLA_NOPE) * knw_ref[...]
            kcat_ref[:, hh * QHEAD:hh * QHEAD + LANES] = kn.astype(kcat_ref.dtype)
            kcat_ref[:, hh * QHEAD + LANES:(hh + 1) * QHEAD] = kp.astype(kcat_ref.dtype)
        v_ref[...] = _dot(cb, wvb_ref[...]).astype(v_ref.dtype)
    else:
        (kp128_ref,) = kv_refs
        kp128_ref[...] = kp


def _front(x, tabs, lw, *, tm, emit_kv, act_dtype):
    n = x.shape[0]
    tper = tabs[0].shape[0]
    nper = tper // tm
    row = lambda i: (i, 0)
    const = lambda i: (0, 0)
    tab_spec = pl.BlockSpec((tm, LANES), lambda i: (i % nper, 0))

    def full(a):
        return pl.BlockSpec(a.shape, const)

    weights = [lw["norm_w"], lw["w_in"], lw["w_ga"], lw["q_a_norm_w"], lw["w_qb"], lw["qn_w"], lw["qp_w"],
               lw["kv_a_norm_w"], lw["kp_w"], lw["w_kb"], lw["kn_w"], lw["w_vb"]]
    in_specs = [pl.BlockSpec((tm, D_MODEL), row)] + [tab_spec] * 4 + [full(w) for w in weights]
    out_shape = [jax.ShapeDtypeStruct((n, 512), BF16)] * 3
    out_shape += [jax.ShapeDtypeStruct((n, 512), BF16)]
    out_shape += [jax.ShapeDtypeStruct((n, 512), act_dtype)]
    out_shape += [jax.ShapeDtypeStruct((n, QCOLS), act_dtype)]
    out_shape += [jax.ShapeDtypeStruct((n, KV_RANK), F32)]
    out_shape += [jax.ShapeDtypeStruct((n, MLA_ROPE), F32)]
    if emit_kv:
        out_shape += [jax.ShapeDtypeStruct((n, QCOLS), BF16)]
        out_shape += [jax.ShapeDtypeStruct((n, MLA_WIDTH), BF16)]
    else:
        out_shape += [jax.ShapeDtypeStruct((n, LANES), F32)]
    out_specs = [pl.BlockSpec((tm, s.shape[1]), row) for s in out_shape]
    return pl.pallas_call(
        functools.partial(_front_kernel, emit_kv, MLA_SCALE * LOG2E if emit_kv else MLA_SCALE),
        grid=(n // tm,),
        in_specs=in_specs,
        out_specs=out_specs,
        out_shape=out_shape,
        compiler_params=pltpu.CompilerParams(dimension_semantics=("arbitrary",),
                                             vmem_limit_bytes=VMEM_LIMIT),
        name="front_kv" if emit_kv else "front",
    )(x, *tabs, *weights)


def _ret_prompt_kernel(sdec, q_ref, k_ref, v_ref, g_ref, intra_ref, qdec_ref, kdec_ref, gnw_ref,
                       o_ref, sout_ref, s_ref):
    c = pl.program_id(1)

    @pl.when(c == 0)
    def _():
        s_ref[...] = jnp.zeros_like(s_ref)

    for b in range(q_ref.shape[0]):
        for hh in range(RET_HEADS):
            sl = slice(hh * RET_DK, (hh + 1) * RET_DK)
            q, k, v = q_ref[b, :, sl], k_ref[b, :, sl], v_ref[b, :, sl]
            s = s_ref[b, hh]
            a = _dot_nt(q, k) * intra_ref[hh]
            o = _dot(a.astype(BF16), v) + _dot(q, s.astype(BF16)) * qdec_ref[:, sl]
            kd = (k.astype(F32) * kdec_ref[:, sl]).astype(BF16)
            s_ref[b, hh] = s * sdec[hh] + _dot_tn(kd, v)
            mu = jnp.mean(o, axis=-1, keepdims=True)
            d = o - mu
            var = jnp.mean(d * d, axis=-1, keepdims=True)
            on = d * lax.rsqrt(var + GN_EPS) * gnw_ref[:, sl]
            o_ref[b, :, sl] = (on * g_ref[b, :, sl].astype(F32)).astype(o_ref.dtype)

    @pl.when(c == pl.num_programs(1) - 1)
    def _():
        sout_ref[...] = s_ref[...]


def _ret_prompt(qr, kr, vr, gr, dec, gnw, *, batch, seq, bb):
    nc = seq // RET_CHUNK
    as3d = lambda a: a.reshape(batch, seq, RET_WIDTH)
    blk = pl.BlockSpec((bb, RET_CHUNK, RET_WIDTH), lambda b, c: (b, c, 0))
    const2 = lambda b, c: (0, 0)
    o, s = pl.pallas_call(
        functools.partial(_ret_prompt_kernel, dec["sdec"]),
        grid=(batch // bb, nc),
        in_specs=[blk, blk, blk, blk,
                  pl.BlockSpec(dec["intra"].shape, lambda b, c: (0, 0, 0)),
                  pl.BlockSpec(dec["qdec"].shape, const2),
                  pl.BlockSpec(dec["kdec"].shape, const2),
                  pl.BlockSpec(gnw.shape, const2)],
        out_specs=[blk, pl.BlockSpec((bb, RET_HEADS, RET_DK, RET_DV), lambda b, c: (b, 0, 0, 0))],
        out_shape=[jax.ShapeDtypeStruct((batch, seq, RET_WIDTH), BF16),
                   jax.ShapeDtypeStruct((batch, RET_HEADS, RET_DK, RET_DV), F32)],
        scratch_shapes=[pltpu.VMEM((bb, RET_HEADS, RET_DK, RET_DV), F32)],
        compiler_params=pltpu.CompilerParams(dimension_semantics=("arbitrary", "arbitrary"),
                                             vmem_limit_bytes=VMEM_LIMIT),
        name="ret_prompt",
    )(as3d(qr), as3d(kr), as3d(vr), as3d(gr), dec["intra"], dec["qdec"], dec["kdec"], gnw)
    return o.reshape(batch * seq, RET_WIDTH), s


def _ret_sample_kernel(sdec, t, q_ref, k_ref, v_ref, g_ref, st_ref, intra_ref, qdec_ref, kdec_ref,
                       gnw_ref, o_ref, sout_ref):
    nseq = st_ref.shape[0]
    for hh in range(RET_HEADS):
        sl = slice(hh * RET_DK, (hh + 1) * RET_DK)
        q, k, v = q_ref[:, sl], k_ref[:, sl], v_ref[:, sl]
        a = _dot_nt(q, k) * intra_ref[hh]
        o_intra = _dot(a.astype(BF16), v)
        q32 = q.astype(F32)
        v32 = v.astype(F32)
        kd32 = (k.astype(F32) * kdec_ref[:, sl]).astype(BF16).astype(F32)
        qdec = qdec_ref[:, sl]
        gnw = gnw_ref[:, sl]
        for s in range(nseq):
            rows = slice(s * t, (s + 1) * t)
            s0 = st_ref[s, hh]
            o = o_intra[rows] + _dot(q32[rows], s0) * qdec[rows]
            sout_ref[s, hh] = s0 * sdec[hh] + _dot_tn(kd32[rows], v32[rows])
            mu = jnp.mean(o, axis=-1, keepdims=True)
            d = o - mu
            var = jnp.mean(d * d, axis=-1, keepdims=True)
            on = d * lax.rsqrt(var + GN_EPS) * gnw
            o_ref[rows, sl] = (on * g_ref[rows, sl].astype(F32)).astype(o_ref.dtype)


def _ret_sample(qr, kr, vr, gr, state, dec, gnw, *, layer, t, nseq):
    n = qr.shape[0]
    nb = n // t
    rows = nseq * t
    blk = pl.BlockSpec((rows, RET_WIDTH), lambda i: (i, 0))
    sblk = pl.BlockSpec((nseq, RET_HEADS, RET_DK, RET_DV), lambda i: (i, 0, 0, 0))
    sblk_in = pl.BlockSpec((None, nseq, RET_HEADS, RET_DK, RET_DV), lambda i: (layer, i, 0, 0, 0))
    const2 = lambda i: (0, 0)
    return pl.pallas_call(
        functools.partial(_ret_sample_kernel, dec["sdec"], t),
        grid=(nb // nseq,),
        in_specs=[blk, blk, blk, blk, sblk_in,
                  pl.BlockSpec(dec["intra"].shape, lambda i: (0, 0, 0)),
                  pl.BlockSpec(dec["qdec"].shape, const2),
                  pl.BlockSpec(dec["kdec"].shape, const2),
                  pl.BlockSpec(gnw.shape, const2)],
        out_specs=[blk, sblk],
        out_shape=[jax.ShapeDtypeStruct(qr.shape, BF16),
                   jax.ShapeDtypeStruct(state.shape[1:], F32)],
        compiler_params=pltpu.CompilerParams(dimension_semantics=("arbitrary",),
                                             vmem_limit_bytes=VMEM_LIMIT),
        name="ret_sample",
    )(qr, kr, vr, gr, state, dec["intra"], dec["qdec"], dec["kdec"], gnw)


def _attn_prompt_kernel(tq, tk, q_ref, k_ref, v_ref, g_ref, o_ref, m_ref, acc_ref):
    qi = pl.program_id(1)
    m_ref[...] = jnp.full(m_ref.shape, NEG_INF, F32)
    acc_ref[...] = jnp.zeros(acc_ref.shape, F32)
    row = qi * tq + lax.broadcasted_iota(jnp.int32, (tq, LANES), 0)
    col = lax.broadcasted_iota(jnp.int32, (tq, LANES), 1)
    ones = jnp.ones((tk, LANES), BF16)
    nslab = tk // LANES

    def score_slabs(hh, start, masked):
        qh = q_ref[:, hh * QHEAD:(hh + 1) * QHEAD]
        kj = k_ref[pl.ds(start, tk), hh * QHEAD:(hh + 1) * QHEAD]
        s = _dot_nt(qh, kj)
        slabs = [s[:, i * LANES:(i + 1) * LANES] for i in range(nslab)]
        if masked:
            slabs = [jnp.where(start + i * LANES + col <= row, sl, NEG_INF)
                     for i, sl in enumerate(slabs)]
        return slabs

    def block(j, masked, st):
        start = pl.multiple_of(j * tk, tk)
        for hh in range(MLA_HEADS):
            slabs = score_slabs(hh, start, masked)
            m_old = m_ref[st, hh]
            m_new = jnp.maximum(m_old, jnp.max(functools.reduce(jnp.maximum, slabs), axis=-1,
                                               keepdims=True))
            alpha = jnp.exp2(m_old - m_new)
            p = jnp.concatenate([jnp.exp2(sl - m_new) for sl in slabs], axis=1).astype(BF16)
            vj = v_ref[pl.ds(start, tk), hh * MLA_DV:(hh + 1) * MLA_DV]
            pv = _dot(p, jnp.concatenate([vj, ones], axis=1))
            acc_ref[st, hh] = acc_ref[st, hh] * jnp.concatenate([alpha, alpha], axis=1) + pv
            m_ref[st, hh] = m_new

    def body(i, carry):
        block(2 * i, False, 0)
        block(2 * i + 1, False, 1)
        return carry

    lax.fori_loop(0, qi // 2, body, 0)

    @pl.when(qi % 2 == 1)
    def _():
        block(qi - 1, False, 0)

    block(qi, True, 1)
    for hh in range(MLA_HEADS):
        m0, m1 = m_ref[0, hh], m_ref[1, hh]
        m = jnp.maximum(m0, m1)
        w0, w1 = jnp.exp2(m0 - m), jnp.exp2(m1 - m)
        acc = (acc_ref[0, hh] * jnp.concatenate([w0, w0], axis=1)
               + acc_ref[1, hh] * jnp.concatenate([w1, w1], axis=1))
        gate = g_ref[:, hh * MLA_DV:(hh + 1) * MLA_DV].astype(F32)
        o_ref[:, hh * MLA_DV:(hh + 1) * MLA_DV] = (acc[:, :MLA_DV] / acc[:, MLA_DV:]
                                                   * gate).astype(o_ref.dtype)


def _attn_prompt(q, kcat, v, ga, *, batch, seq, tq, tk):
    assert tq == tk, "one diagonal key block per query tile"
    nq = seq // tq
    return pl.pallas_call(
        functools.partial(_attn_prompt_kernel, tq, tk),
        grid=(batch, nq),
        in_specs=[pl.BlockSpec((tq, QCOLS), lambda b, i: (b * nq + i, 0)),
                  pl.BlockSpec((seq, QCOLS), lambda b, i: (b, 0)),
                  pl.BlockSpec((seq, MLA_WIDTH), lambda b, i: (b, 0)),
                  pl.BlockSpec((tq, MLA_WIDTH), lambda b, i: (b * nq + i, 0))],
        out_specs=pl.BlockSpec((tq, MLA_WIDTH), lambda b, i: (b * nq + i, 0)),
        out_shape=jax.ShapeDtypeStruct((q.shape[0], MLA_WIDTH), BF16),
        scratch_shapes=[pltpu.VMEM((2, MLA_HEADS, tq, LANES), F32),
                        pltpu.VMEM((2, MLA_HEADS, tq, 2 * MLA_DV), F32)],
        compiler_params=pltpu.CompilerParams(dimension_semantics=("arbitrary", "arbitrary"),
                                             vmem_limit_bytes=VMEM_LIMIT),
        name="attn_prompt",
    )(q, kcat, v, ga)


NBUF = 3


def _col_from_row(row_vec, eye):
    return jnp.sum(jnp.where(eye, jnp.broadcast_to(row_vec, eye.shape), 0.0), axis=1,
                   keepdims=True)


def _page_copies(layer, ppb, pt_ref, ckv_hbm, kpet_hbm, cbuf, kpbuf, sems, step, slot):
    copies = []
    for j in range(ppb):
        page = pt_ref[step * ppb + j]
        rows = pl.ds(j * PAGE_SIZE, PAGE_SIZE)
        copies.append(pltpu.make_async_copy(ckv_hbm.at[layer, page], cbuf.at[slot, rows],
                                            sems.at[slot, 0]))
        copies.append(pltpu.make_async_copy(kpet_hbm.at[layer, page],
                                            kpbuf.at[slot, pl.ds(0, MLA_ROPE), rows],
                                            sems.at[slot, 1]))
    return copies


def _start_all(copies):
    for i, cp in enumerate(copies):
        cp.start(priority=(i // 2) % 2)


def _attn_sample_kernel(layer, ppb, t, nblk, pt_ref, q_ref, cnew_ref, kpnew_ref, g_ref, wkbp_ref,
                        wkb_ref, wvb_ref, knw_ref, ones_ref, ckv_hbm, kpet_hbm, o_ref,
                        cbuf, kpbuf, sems, cb_ref, sn_ref, r2_ref, m_ref, l_ref, acc_ref, qtil_ref,
                        b2t_ref):
    b = pl.program_id(0)
    nseq = pl.num_programs(0) - 1
    nstep = nseq * nblk
    group = MLA_HEADS * t
    nsub = LANES // group
    rows = cbuf.shape[1]
    sub = rows // nsub
    copies = functools.partial(_page_copies, layer, ppb, pt_ref, ckv_hbm, kpet_hbm, cbuf, kpbuf,
                               sems)

    ri = lax.broadcasted_iota(jnp.int32, (LANES, LANES), 0)
    ci = lax.broadcasted_iota(jnp.int32, (LANES, LANES), 1)
    eye = ri == ci

    def build_operands():
        q = q_ref[...]
        qn = jnp.concatenate([q[:, hh * QHEAD:hh * QHEAD + LANES] for hh in range(MLA_HEADS)],
                             axis=1) * jnp.concatenate([knw_ref[...]] * MLA_HEADS, axis=1)
        qn = jnp.concatenate([qn] * (LANES // t), axis=0)
        rh = (lax.broadcasted_iota(jnp.int32, qn.shape, 0) % group) // t
        lh = lax.broadcasted_iota(jnp.int32, qn.shape, 1) // MLA_NOPE
        qt = jnp.where(rh == lh, qn, 0.0).astype(BF16)
        qtil = _dot_nt(wkb_ref[...], qt)
        lane_g = lax.broadcasted_iota(jnp.int32, qtil.shape, 1) // group
        qp = jnp.concatenate([q[:, hh * QHEAD + LANES:(hh + 1) * QHEAD] for hh in range(MLA_HEADS)],
                             axis=0)
        qp = jnp.concatenate([qp] * nsub, axis=0)
        row_g = lax.broadcasted_iota(jnp.int32, qp.shape, 0) // group
        for g in range(nsub):
            qtil_ref[g] = jnp.where(lane_g == g, qtil, 0.0).astype(BF16)
            b2t_ref[g, LANES:, :] = jnp.concatenate(
                [jnp.zeros(qp.shape, F32), jnp.where(row_g == g, qp, 0.0)], axis=1).astype(BF16)

    @pl.when(b == 0)
    def _():
        kpbuf[:, MLA_ROPE:, :] = jnp.zeros((NBUF, LANES - MLA_ROPE, rows), F32)
        for g in range(nsub):
            b2t_ref[g, :LANES, :] = ones_ref[g]
        last = (nblk - 1) % 2
        sn_ref[last] = jnp.zeros(sn_ref.shape[1:], F32)
        r2_ref[last] = jnp.zeros(r2_ref.shape[1:], F32)
        cb_ref[last] = jnp.zeros(cb_ref.shape[1:], BF16)
        m_ref[...] = jnp.full(m_ref.shape, NEG_INF, F32)
        l_ref[...] = jnp.zeros(l_ref.shape, F32)
        acc_ref[...] = jnp.zeros(acc_ref.shape, F32)
        for blk in range(NBUF - 1):
            _start_all(copies(blk, blk))

    def scores(c_bf, kp_pad, g):
        kf = _dot(c_bf, wkbp_ref[...])
        sq = kf * kf
        fold = (sq[:, 0:LANES] + sq[:, LANES:2 * LANES]) + (sq[:, 2 * LANES:3 * LANES]
                                                            + sq[:, 3 * LANES:4 * LANES])
        lhs2 = jnp.concatenate([fold.astype(BF16), kp_pad.astype(BF16)], axis=1)
        return _dot(c_bf, qtil_ref[g]), _dot_nt(lhs2, b2t_ref[g])

    def softmax_step(sn, r2, valid):
        s = sn * lax.rsqrt(r2[:, :LANES] * (1.0 / MLA_NOPE) + EPS) + r2[:, LANES:]
        if valid is not None:
            s = jnp.where(valid, s, NEG_INF)
        m_old = m_ref[0:1, :]
        m_new = jnp.maximum(m_old, jnp.max(s, axis=0, keepdims=True))
        alpha = jnp.exp(m_old - m_new)
        p = jnp.exp(s - m_new)
        if valid is not None:
            p = jnp.where(valid, p, 0.0)
        l_ref[0:1, :] = l_ref[0:1, :] * alpha + jnp.sum(p, axis=0, keepdims=True)
        m_ref[0:1, :] = m_new
        return p, _col_from_row(alpha, eye)

    def finish_sequence():
        m_col = _col_from_row(m_ref[0:1, :], eye)
        l_col = _col_from_row(l_ref[0:1, :], eye)
        parts = [slice(g * group, (g + 1) * group) for g in range(nsub)]
        m_fin = functools.reduce(jnp.maximum, [m_col[sl] for sl in parts])
        w = [jnp.exp(m_col[sl] - m_fin) for sl in parts]
        l_fin = sum(w[g] * l_col[parts[g]] for g in range(nsub))
        lat = sum(w[g] * acc_ref[parts[g], :] for g in range(nsub)) / l_fin
        full = _dot(lat.astype(BF16), wvb_ref[...])
        o = jnp.concatenate([full[hh * t:(hh + 1) * t, hh * MLA_DV:(hh + 1) * MLA_DV]
                             for hh in range(MLA_HEADS)], axis=1)
        o_ref[...] = (o * g_ref[...]).astype(o_ref.dtype)

    for k in range(nblk):
        v = b * nblk + k
        cur, prv = k % 2, (k + 1) % 2
        ring = v % NBUF
        for cp in copies(jnp.minimum(v, nstep - 1), ring):
            cp.wait()
        _start_all(copies(jnp.minimum(v + NBUF - 1, nstep - 1), (v + NBUF - 1) % NBUF))
        if k == 0:
            build_operands()

        p, alpha_col = softmax_step(sn_ref[prv], r2_ref[prv], None)
        lane_g = lax.broadcasted_iota(jnp.int32, p.shape, 1) // group
        p_rows = jnp.concatenate([jnp.where(lane_g == g, p, 0.0) for g in range(nsub)], axis=0)
        acc_ref[...] = acc_ref[...] * alpha_col + _dot_tn(p_rows.astype(BF16), cb_ref[prv])

        sn = r2 = None
        for g in range(nsub):
            rs = pl.ds(g * sub, sub)
            c_g = cbuf[ring, rs, :].astype(BF16)
            cb_ref[cur, rs, :] = c_g
            sn_g, r2_g = scores(c_g, kpbuf[ring, :, rs].T, g)
            sn = sn_g if sn is None else sn + sn_g
            r2 = r2_g if r2 is None else r2 + r2_g
        sn_ref[cur] = sn
        r2_ref[cur] = r2

        if k == 0:
            pad = 2 * SUBLANES - t
            c_new = jnp.concatenate([cnew_ref[...], jnp.zeros((pad, KV_RANK), F32)],
                                    axis=0).astype(BF16)
            kp_new = jnp.concatenate([kpnew_ref[...], jnp.zeros((pad, LANES), F32)], axis=0)
            sn0, r20 = scores(c_new, kp_new, 0)
            pl.when(b > 0)(finish_sequence)
            m_ref[...] = jnp.full(m_ref.shape, NEG_INF, F32)
            l_ref[...] = jnp.zeros(l_ref.shape, F32)
            r = lax.broadcasted_iota(jnp.int32, (2 * SUBLANES, LANES), 0)
            j = lax.broadcasted_iota(jnp.int32, (2 * SUBLANES, LANES), 1)
            p0, _ = softmax_step(sn0, r20, (r < t) & (r <= j % t) & (j < group))
            acc_ref[...] = _dot_tn(p0.astype(BF16), c_new)

    @pl.when(b == nseq)
    def _():
        v_last = b * nblk + nblk - 1
        for k in range(1, NBUF):
            for cp in copies(nstep - 1, (v_last + k) % NBUF):
                cp.wait()


def _attn_sample(page_table, q, c_new, kp_new, ga, w_kb, w_vb, kn_w, cache_ckv, cache_kpet, *,
                 layer, t, ppb):
    nseq, npages = page_table.shape
    nblk = npages // ppb
    rows = ppb * PAGE_SIZE
    group = MLA_HEADS * t
    nsub = LANES // group
    fold = MLA_NOPE // nsub
    w_kbp = w_kb.reshape(KV_RANK, MLA_HEADS, nsub, fold).transpose(0, 2, 1, 3).reshape(w_kb.shape)
    j = np.arange(LANES)
    d = np.arange(2 * LANES)
    ones_top = ((d[None, None, :] < LANES)
                & (d[None, None, :] // fold == ((j % group) // t)[None, :, None])
                & (j[None, :, None] // group == np.arange(nsub)[:, None, None]))
    ones_top = jnp.asarray(ones_top, BF16)
    assert nblk % 2 == 0 and nseq * nblk >= NBUF, "scratch parity is static per block of a sequence"
    seq_now = lambda b, pt: (jnp.minimum(b, nseq - 1), 0)
    seq_prev = lambda b, pt: (jnp.maximum(b - 1, 0), 0)
    const = lambda b, pt: (0, 0)
    sub = rows // nsub
    grid_spec = pltpu.PrefetchScalarGridSpec(
        num_scalar_prefetch=1,
        grid=(nseq + 1,),
        in_specs=[pl.BlockSpec((t, QCOLS), seq_now),
                  pl.BlockSpec((t, KV_RANK), seq_now),
                  pl.BlockSpec((t, LANES), seq_now),
                  pl.BlockSpec((t, MLA_WIDTH), seq_prev),
                  pl.BlockSpec(w_kbp.shape, const),
                  pl.BlockSpec(w_kb.shape, const),
                  pl.BlockSpec(w_vb.shape, const),
                  pl.BlockSpec(kn_w.shape, const),
                  pl.BlockSpec(ones_top.shape, lambda s, pt: (0, 0, 0)),
                  pl.BlockSpec(memory_space=pl.ANY),
                  pl.BlockSpec(memory_space=pl.ANY)],
        out_specs=pl.BlockSpec((t, MLA_WIDTH), seq_prev),
        scratch_shapes=[pltpu.VMEM((NBUF, rows, KV_RANK), F32),
                        pltpu.VMEM((NBUF, LANES, rows), F32),
                        pltpu.SemaphoreType.DMA((NBUF, 2)),
                        pltpu.VMEM((2, rows, KV_RANK), BF16),
                        pltpu.VMEM((2, sub, LANES), F32),
                        pltpu.VMEM((2, sub, 2 * LANES), F32),
                        pltpu.VMEM((SUBLANES, LANES), F32),
                        pltpu.VMEM((SUBLANES, LANES), F32),
                        pltpu.VMEM((LANES, KV_RANK), F32),
                        pltpu.VMEM((nsub, KV_RANK, LANES), BF16),
                        pltpu.VMEM((nsub, 2 * LANES, 2 * LANES), BF16)])
    return pl.pallas_call(
        functools.partial(_attn_sample_kernel, layer, ppb, t, nblk),
        grid_spec=grid_spec,
        out_shape=jax.ShapeDtypeStruct((nseq * t, MLA_WIDTH), F32),
        compiler_params=pltpu.CompilerParams(dimension_semantics=("arbitrary",),
                                             vmem_limit_bytes=VMEM_LIMIT),
        name="attn_sample",
    )(page_table.reshape(-1), q, c_new, kp_new, ga, w_kbp, w_kb, w_vb, kn_w, ones_top, cache_ckv,
      cache_kpet)


def _outproj_kernel(or_ref, oa_ref, x_ref, w_ref, y_ref):
    y = _dot(or_ref[...].astype(BF16), w_ref[:RET_WIDTH, :])
    y = y + _dot(oa_ref[...].astype(BF16), w_ref[RET_WIDTH:, :])
    y_ref[...] = x_ref[...] + y


def _outproj(o_r, o_a, x, w_out, *, tm):
    n = x.shape[0]
    row = lambda i: (i, 0)
    return pl.pallas_call(
        _outproj_kernel,
        grid=(n // tm,),
        in_specs=[pl.BlockSpec((tm, RET_WIDTH), row), pl.BlockSpec((tm, MLA_WIDTH), row),
                  pl.BlockSpec((tm, D_MODEL), row), pl.BlockSpec(w_out.shape, lambda i: (0, 0))],
        out_specs=pl.BlockSpec((tm, D_MODEL), row),
        out_shape=jax.ShapeDtypeStruct(x.shape, F32),
        compiler_params=pltpu.CompilerParams(dimension_semantics=("arbitrary",),
                                             vmem_limit_bytes=VMEM_LIMIT),
        name="outproj",
    )(o_r, o_a, x, w_out)


def _rope_tables(pos):
    def cs(dim):
        inv = ROPE_BASE ** (-(jnp.arange(0, dim, 2, dtype=F32) / dim))
        ang = pos.astype(F32)[:, None] * inv[None, :]
        return jnp.cos(ang), jnp.sin(ang)

    cr, sr = cs(RET_DK)
    ca, sa = cs(MLA_ROPE)
    zeros = jnp.zeros((pos.shape[0], LANES - MLA_ROPE), F32)
    return (jnp.concatenate([cr, cr], axis=1), jnp.concatenate([-sr, sr], axis=1),
            jnp.concatenate([ca, ca, zeros], axis=1), jnp.concatenate([-sa, sa, zeros], axis=1))


def _decay_tables(chunk, nseq):
    log_g = jnp.log1p(-jnp.exp2(-5.0 - jnp.arange(RET_HEADS, dtype=F32)))
    i = jnp.arange(chunk, dtype=F32)
    diff = i[:, None] - i[None, :]
    intra = jnp.where(diff >= 0, jnp.exp(log_g[:, None, None] * jnp.maximum(diff, 0.0)), 0.0)
    if nseq > 1:
        same = jnp.eye(nseq, dtype=F32)
        intra = jnp.einsum("ab,hij->haibj", same, intra).reshape(RET_HEADS, nseq * chunk,
                                                                 nseq * chunk)
    q_dec = jnp.exp(log_g[None, :] * (i[:, None] + 1.0))
    k_dec = jnp.exp(log_g[None, :] * (chunk - 1.0 - i[:, None]))
    widen = lambda a: jnp.tile(jnp.repeat(a, RET_DK, axis=1), (nseq, 1))
    g = 1.0 - np.exp2(-5.0 - np.arange(RET_HEADS, dtype=np.float64))
    sdec = tuple(float(v) for v in np.exp(np.log(g) * chunk))
    return {"intra": intra, "qdec": widen(q_dec), "kdec": widen(k_dec), "sdec": sdec}


def _prep_weights(norm_w, w_in, q_a_norm_w, w_qb, qn_w, qp_w, kv_a_norm_w, kp_w, w_kb, kn_w, w_vb,
                  ret_gn_w, w_out):
    depth = w_in.shape[0]
    s = np.cumsum((512, 512, 512, 512, Q_RANK, KV_RANK, MLA_ROPE, MLA_WIDTH))
    w_in_r = jnp.pad(w_in[:, :, :s[6]], ((0, 0), (0, 0), (0, LANES - MLA_ROPE))).astype(BF16)
    w_ga = w_in[:, :, s[6]:].astype(BF16)
    wq = w_qb.reshape(depth, Q_RANK, MLA_HEADS, MLA_NOPE + MLA_ROPE)
    wq = jnp.pad(wq, ((0, 0), (0, 0), (0, 0), (0, QHEAD - MLA_NOPE - MLA_ROPE)))
    wq = wq.reshape(depth, Q_RANK, QCOLS).astype(BF16)
    pad_rope = lambda w: jnp.pad(w, ((0, 0), (0, LANES - MLA_ROPE)))
    layers = []
    for l in range(depth):
        layers.append({
            "norm_w": norm_w[l][None], "w_in": w_in_r[l], "w_ga": w_ga[l], "q_a_norm_w": q_a_norm_w[l][None],
            "w_qb": wq[l], "qn_w": qn_w[l][None], "qp_w": pad_rope(qp_w)[l][None],
            "kv_a_norm_w": kv_a_norm_w[l][None], "kp_w": pad_rope(kp_w)[l][None],
            "w_kb": w_kb[l].astype(BF16), "kn_w": kn_w[l][None], "w_vb": w_vb[l].astype(BF16),
            "ret_gn_w": ret_gn_w[l][None], "w_out": w_out[l].astype(BF16)})
    return layers


def kernel(x_prompt, x_sample, cache_ckv, cache_kpe, state_ret, page_table, norm_w, w_in, q_a_norm_w,
           w_qb, qn_w, qp_w, kv_a_norm_w, kp_w, w_kb, kn_w, w_vb, ret_gn_w, w_out):
    b_p, t_p, _ = x_prompt.shape
    b_s, t_s, _ = x_sample.shape
    depth = w_in.shape[0]
    past = page_table.shape[1] * PAGE_SIZE
    tm_p = 512
    tm_s = 512
    ret_nseq = 8
    ret_bb = 4
    ppb = 32
    cache_kpet = jnp.swapaxes(cache_kpe, 2, 3)

    layers = _prep_weights(norm_w, w_in, q_a_norm_w, w_qb, qn_w, qp_w, kv_a_norm_w, kp_w, w_kb, kn_w,
                           w_vb, ret_gn_w, w_out)
    tabs_p = _rope_tables(jnp.arange(t_p, dtype=jnp.int32))
    tabs_s = _rope_tables(past + jnp.arange(t_s, dtype=jnp.int32))
    tabs_s = tuple(jnp.tile(a, (tm_s // t_s, 1)) for a in tabs_s)
    dec_p = _decay_tables(RET_CHUNK, 1)
    dec_s = _decay_tables(t_s, ret_nseq)

    hp = x_prompt.reshape(b_p * t_p, D_MODEL)
    hs = x_sample.reshape(b_s * t_s, D_MODEL)
    outs = {k: [] for k in ("ckv_p", "kpe_p", "ret_p", "ckv_s", "kpe_s", "ret_s")}
    for l in range(depth):
        lw = layers[l]
        qr, kr, vr, gr, ga, q, c, kpe, kcat, v = _front(hp, tabs_p, lw, tm=tm_p, emit_kv=True,
                                                        act_dtype=BF16)
        o_r, s_new = _ret_prompt(qr, kr, vr, gr, dec_p, lw["ret_gn_w"], batch=b_p, seq=t_p,
                                 bb=ret_bb)
        o_a = _attn_prompt(q, kcat, v, ga, batch=b_p, seq=t_p, tq=256, tk=256)
        hp = _outproj(o_r, o_a, hp, lw["w_out"], tm=tm_p)
        outs["ckv_p"].append(c.reshape(b_p, t_p, KV_RANK))
        outs["kpe_p"].append(kpe.reshape(b_p, t_p, MLA_ROPE))
        outs["ret_p"].append(s_new)

        qr, kr, vr, gr, ga, q, c, kpe, kp128 = _front(hs, tabs_s, lw, tm=tm_s, emit_kv=False,
                                                      act_dtype=F32)
        o_r, s_new = _ret_sample(qr, kr, vr, gr, state_ret, dec_s, lw["ret_gn_w"], layer=l, t=t_s,
                                 nseq=ret_nseq)
        o_a = _attn_sample(page_table, q, c, kp128, ga, lw["w_kb"], lw["w_vb"], lw["kn_w"],
                           cache_ckv, cache_kpet, layer=l, t=t_s, ppb=ppb)
        hs = _outproj(o_r, o_a, hs, lw["w_out"], tm=tm_s)
        outs["ckv_s"].append(c.reshape(b_s, t_s, KV_RANK))
        outs["kpe_s"].append(kpe.reshape(b_s, t_s, MLA_ROPE))
        outs["ret_s"].append(s_new)

    return (hp.reshape(b_p, t_p, D_MODEL), hs.reshape(b_s, t_s, D_MODEL),
            jnp.stack(outs["ckv_p"]), jnp.stack(outs["kpe_p"]), jnp.stack(outs["ret_p"]),
            jnp.stack(outs["ckv_s"]), jnp.stack(outs["kpe_s"]), jnp.stack(outs["ret_s"]))
```

```python
import functools

import numpy as np
import jax
import jax.numpy as jnp
from jax import lax
from jax.experimental import pallas as pl
from jax.experimental.pallas import tpu as pltpu

F32 = jnp.float32
BF16 = jnp.bfloat16

D_MODEL = 1024
PAGE_SIZE = 128
RET_HEADS = 4
RET_DK = 128
RET_DV = 128
RET_WIDTH = RET_HEADS * RET_DV
RET_CHUNK = 256
MLA_HEADS = 4
MLA_NOPE = 128
MLA_ROPE = 64
MLA_DV = 128
MLA_WIDTH = MLA_HEADS * MLA_DV
Q_RANK = 384
KV_RANK = 256
MLA_SCALE = (MLA_NOPE + MLA_ROPE) ** -0.5
LOG2E = 1.4426950408889634
ROPE_BASE = 10000.0
EPS = 1e-6
GN_EPS = 1e-5
NEG_INF = -1e30

LANES = 128
SUBLANES = 8
VMEM_LIMIT = 56 * 1024 * 1024

OFF_QR = 0
OFF_KR = 512
OFF_VR = 1024
OFF_GR = 1536
OFF_QL = 2048
OFF_C = OFF_QL + Q_RANK
OFF_KPE = OFF_C + KV_RANK
IN_COLS_PAD = OFF_KPE + LANES
QHEAD = 2 * LANES
QCOLS = MLA_HEADS * QHEAD

NT_DIMS = (((1,), (1,)), ((), ()))
TN_DIMS = (((0,), (0,)), ((), ()))


def _dot(a, b):
    return jnp.dot(a, b, preferred_element_type=F32)


def _dot_nt(a, b):
    return lax.dot_general(a, b, NT_DIMS, preferred_element_type=F32)


def _dot_tn(a, b):
    return lax.dot_general(a, b, TN_DIMS, preferred_element_type=F32)


def _rms_scale(v, n):
    return lax.rsqrt(jnp.sum(v * v, axis=-1, keepdims=True) * (1.0 / n) + EPS)


def _rope64(u, cos, sin, lane):
    swapped = jnp.where(lane < MLA_ROPE // 2,
                        pltpu.roll(u, LANES - MLA_ROPE // 2, 1),
                        pltpu.roll(u, MLA_ROPE // 2, 1))
    return u * cos + swapped * sin


def _front_kernel(emit_kv, qscale, x_ref, cosr_ref, sinr_ref, cosa_ref, sina_ref, normw_ref, win_ref, wga_ref,
                  qanw_ref, wqb_ref, qnw_ref, qpw_ref, kvnw_ref, kpw_ref, wkb_ref, knw_ref,
                  wvb_ref, qr_ref, kr_ref, vr_ref, gr_ref, ga_ref, q_ref, c_ref, kpe_ref,
                  *kv_refs):
    x = x_ref[...]
    h = (x * _rms_scale(x, D_MODEL) * normw_ref[...]).astype(BF16)
    cosr, sinr = cosr_ref[...], sinr_ref[...]
    cosa, sina = cosa_ref[...], sina_ref[...]
    lane = lax.broadcasted_iota(jnp.int32, cosa.shape, 1)

    half = 2 * RET_DK
    for c0 in range(0, RET_WIDTH, half):
        cs = slice(c0, c0 + half)
        zq = _dot(h, win_ref[:, OFF_QR + c0:OFF_QR + c0 + half])
        zk = _dot(h, win_ref[:, OFF_KR + c0:OFF_KR + c0 + half])
        for d0 in range(0, half, RET_DK):
            sl = slice(c0 + d0, c0 + d0 + RET_DK)
            qh, kh = zq[:, d0:d0 + RET_DK], zk[:, d0:d0 + RET_DK]
            qr_ref[:, sl] = (qh * cosr + pltpu.roll(qh, RET_DK // 2, 1) * sinr).astype(qr_ref.dtype)
            kr_ref[:, sl] = ((kh * cosr + pltpu.roll(kh, RET_DK // 2, 1) * sinr)
                             * (RET_DK ** -0.5)).astype(kr_ref.dtype)
        vr_ref[:, cs] = _dot(h, win_ref[:, OFF_VR + c0:OFF_VR + c0 + half]).astype(vr_ref.dtype)
        zg = _dot(h, win_ref[:, OFF_GR + c0:OFF_GR + c0 + half])
        gr_ref[:, cs] = (zg * jax.nn.sigmoid(zg)).astype(gr_ref.dtype)
        zg = _dot(h, wga_ref[:, cs])
        ga_ref[:, cs] = (zg * jax.nn.sigmoid(zg)).astype(ga_ref.dtype)

    zql = _dot(h, win_ref[:, OFF_QL:OFF_QL + Q_RANK])
    qa = (zql * _rms_scale(zql, Q_RANK) * qanw_ref[...]).astype(BF16)
    q = _dot(qa, wqb_ref[...])
    for hh in range(MLA_HEADS):
        base = hh * QHEAD
        qn = q[:, base:base + LANES]
        qn = qn * _rms_scale(qn, MLA_NOPE) * qnw_ref[...] * qscale
        qp = q[:, base + LANES:base + QHEAD]
        u = qp * _rms_scale(qp, MLA_ROPE) * qpw_ref[...]
        qp = _rope64(u, cosa, sina, lane) * qscale
        q_ref[:, base:base + LANES] = qn.astype(q_ref.dtype)
        q_ref[:, base + LANES:base + QHEAD] = qp.astype(q_ref.dtype)

    zck = _dot(h, win_ref[:, OFF_C:OFF_C + KV_RANK + LANES])
    zc = zck[:, :KV_RANK]
    c = zc * _rms_scale(zc, KV_RANK) * kvnw_ref[...]
    c_ref[...] = c
    zp = zck[:, KV_RANK:]
    u = zp * _rms_scale(zp, MLA_ROPE) * kpw_ref[...]
    kp = _rope64(u, cosa, sina, lane)
    kpe_ref[...] = kp[:, :MLA_ROPE]

    if emit_kv:
        kcat_ref, v_ref = kv_refs
        cb = c.astype(BF16)
        kf = _dot(cb, wkb_ref[...])
        for hh in range(MLA_HEADS):
            kn = kf[:, hh * MLA_NOPE:(hh + 1) * MLA_NOPE]
            kn = kn * _rms_scale(kn, MLA_NOPE) * knw_ref[...]
            kcat_ref[:, hh * QHEAD:hh * QHEAD + LANES] = kn.astype(kcat_ref.dtype)
            kcat_ref[:, hh * QHEAD + LANES:(hh + 1) * QHEAD] = kp.astype(kcat_ref.dtype)
        v_ref[...] = _dot(cb, wvb_ref[...]).astype(v_ref.dtype)
    else:
        (kp128_ref,) = kv_refs
        kp128_ref[...] = kp


def _front(x, tabs, lw, *, tm, emit_kv, act_dtype):
    n = x.shape[0]
    tper = tabs[0].shape[0]
    nper = tper // tm
    row = lambda i: (i, 0)
    const = lambda i: (0, 0)
    tab_spec = pl.BlockSpec((tm, LANES), lambda i: (i % nper, 0))

    def full(a):
        return pl.BlockSpec(a.shape, const)

    weights = [lw["norm_w"], lw["w_in"], lw["w_ga"], lw["q_a_norm_w"], lw["w_qb"], lw["qn_w"], lw["qp_w"],
               lw["kv_a_norm_w"], lw["kp_w"], lw["w_kb"], lw["kn_w"], lw["w_vb"]]
    in_specs = [pl.BlockSpec((tm, D_MODEL), row)] + [tab_spec] * 4 + [full(w) for w in weights]
    out_shape = [jax.ShapeDtypeStruct((n, 512), BF16)] * 3
    out_shape += [jax.ShapeDtypeStruct((n, 512), BF16)]
    out_shape += [jax.ShapeDtypeStruct((n, 512), act_dtype)]
    out_shape += [jax.ShapeDtypeStruct((n, QCOLS), act_dtype)]
    out_shape += [jax.ShapeDtypeStruct((n, KV_RANK), F32)]
    out_shape += [jax.ShapeDtypeStruct((n, MLA_ROPE), F32)]
    if emit_kv:
        out_shape += [jax.ShapeDtypeStruct((n, QCOLS), BF16)]
        out_shape += [jax.ShapeDtypeStruct((n, MLA_WIDTH), BF16)]
    else:
        out_shape += [jax.ShapeDtypeStruct((n, LANES), F32)]
    out_specs = [pl.BlockSpec((tm, s.shape[1]), row) for s in out_shape]
    return pl.pallas_call(
        functools.partial(_front_kernel, emit_kv, MLA_SCALE * LOG2E if emit_kv else MLA_SCALE),
        grid=(n // tm,),
        in_specs=in_specs,
        out_specs=out_specs,
        out_shape=out_shape,
        compiler_params=pltpu.CompilerParams(dimension_semantics=("arbitrary",),
                                             vmem_limit_bytes=VMEM_LIMIT),
        name="front_kv" if emit_kv else "front",
    )(x, *tabs, *weights)


def _ret_prompt_kernel(sdec, q_ref, k_ref, v_ref, g_ref, intra_ref, qdec_ref, kdec_ref, gnw_ref,
                       o_ref, sout_ref, s_ref):
    c = pl.program_id(1)

    @pl.when(c == 0)
    def _():
        s_ref[...] = jnp.zeros_like(s_ref)

    for b in range(q_ref.shape[0]):
        for hh in range(RET_HEADS):
            sl = slice(hh * RET_DK, (hh + 1) * RET_DK)
            q, k, v = q_ref[b, :, sl], k_ref[b, :, sl], v_ref[b, :, sl]
            s = s_ref[b, hh]
            a = _dot_nt(q, k) * intra_ref[hh]
            o = _dot(a.astype(BF16), v) + _dot(q, s.astype(BF16)) * qdec_ref[:, sl]
            kd = (k.astype(F32) * kdec_ref[:, sl]).astype(BF16)
            s_ref[b, hh] = s * sdec[hh] + _dot_tn(kd, v)
            mu = jnp.mean(o, axis=-1, keepdims=True)
            d = o - mu
            var = jnp.mean(d * d, axis=-1, keepdims=True)
            on = d * lax.rsqrt(var + GN_EPS) * gnw_ref[:, sl]
            o_ref[b, :, sl] = (on * g_ref[b, :, sl].astype(F32)).astype(o_ref.dtype)

    @pl.when(c == pl.num_programs(1) - 1)
    def _():
        sout_ref[...] = s_ref[...]


def _ret_prompt(qr, kr, vr, gr, dec, gnw, *, batch, seq, bb):
    nc = seq // RET_CHUNK
    as3d = lambda a: a.reshape(batch, seq, RET_WIDTH)
    blk = pl.BlockSpec((bb, RET_CHUNK, RET_WIDTH), lambda b, c: (b, c, 0))
    const2 = lambda b, c: (0, 0)
    o, s = pl.pallas_call(
        functools.partial(_ret_prompt_kernel, dec["sdec"]),
        grid=(batch // bb, nc),
        in_specs=[blk, blk, blk, blk,
                  pl.BlockSpec(dec["intra"].shape, lambda b, c: (0, 0, 0)),
                  pl.BlockSpec(dec["qdec"].shape, const2),
                  pl.BlockSpec(dec["kdec"].shape, const2),
                  pl.BlockSpec(gnw.shape, const2)],
        out_specs=[blk, pl.BlockSpec((bb, RET_HEADS, RET_DK, RET_DV), lambda b, c: (b, 0, 0, 0))],
        out_shape=[jax.ShapeDtypeStruct((batch, seq, RET_WIDTH), BF16),
                   jax.ShapeDtypeStruct((batch, RET_HEADS, RET_DK, RET_DV), F32)],
        scratch_shapes=[pltpu.VMEM((bb, RET_HEADS, RET_DK, RET_DV), F32)],
        compiler_params=pltpu.CompilerParams(dimension_semantics=("arbitrary", "arbitrary"),
                                             vmem_limit_bytes=VMEM_LIMIT),
        name="ret_prompt",
    )(as3d(qr), as3d(kr), as3d(vr), as3d(gr), dec["intra"], dec["qdec"], dec["kdec"], gnw)
    return o.reshape(batch * seq, RET_WIDTH), s


def _ret_sample_kernel(sdec, t, q_ref, k_ref, v_ref, g_ref, st_ref, intra_ref, qdec_ref, kdec_ref,
                       gnw_ref, o_ref, sout_ref):
    nseq = st_ref.shape[0]
    for hh in range(RET_HEADS):
        sl = slice(hh * RET_DK, (hh + 1) * RET_DK)
        q, k, v = q_ref[:, sl], k_ref[:, sl], v_ref[:, sl]
        a = _dot_nt(q, k) * intra_ref[hh]
        o_intra = _dot(a.astype(BF16), v)
        q32 = q.astype(F32)
        v32 = v.astype(F32)
        kd32 = (k.astype(F32) * kdec_ref[:, sl]).astype(BF16).astype(F32)
        qdec = qdec_ref[:, sl]
        gnw = gnw_ref[:, sl]
        for s in range(nseq):
            rows = slice(s * t, (s + 1) * t)
            s0 = st_ref[s, hh]
            o = o_intra[rows] + _dot(q32[rows], s0) * qdec[rows]
            sout_ref[s, hh] = s0 * sdec[hh] + _dot_tn(kd32[rows], v32[rows])
            mu = jnp.mean(o, axis=-1, keepdims=True)
            d = o - mu
            var = jnp.mean(d * d, axis=-1, keepdims=True)
            on = d * lax.rsqrt(var + GN_EPS) * gnw
            o_ref[rows, sl] = (on * g_ref[rows, sl].astype(F32)).astype(o_ref.dtype)


def _ret_sample(qr, kr, vr, gr, state, dec, gnw, *, layer, t, nseq):
    n = qr.shape[0]
    nb = n // t
    rows = nseq * t
    blk = pl.BlockSpec((rows, RET_WIDTH), lambda i: (i, 0))
    sblk = pl.BlockSpec((nseq, RET_HEADS, RET_DK, RET_DV), lambda i: (i, 0, 0, 0))
    sblk_in = pl.BlockSpec((None, nseq, RET_HEADS, RET_DK, RET_DV), lambda i: (layer, i, 0, 0, 0))
    const2 = lambda i: (0, 0)
    return pl.pallas_call(
        functools.partial(_ret_sample_kernel, dec["sdec"], t),
        grid=(nb // nseq,),
        in_specs=[blk, blk, blk, blk, sblk_in,
                  pl.BlockSpec(dec["intra"].shape, lambda i: (0, 0, 0)),
                  pl.BlockSpec(dec["qdec"].shape, const2),
                  pl.BlockSpec(dec["kdec"].shape, const2),
                  pl.BlockSpec(gnw.shape, const2)],
        out_specs=[blk, sblk],
        out_shape=[jax.ShapeDtypeStruct(qr.shape, BF16),
                   jax.ShapeDtypeStruct(state.shape[1:], F32)],
        compiler_params=pltpu.CompilerParams(dimension_semantics=("arbitrary",),
                                             vmem_limit_bytes=VMEM_LIMIT),
        name="ret_sample",
    )(qr, kr, vr, gr, state, dec["intra"], dec["qdec"], dec["kdec"], gnw)


def _attn_prompt_kernel(tq, tk, q_ref, k_ref, v_ref, g_ref, o_ref, m_ref, acc_ref):
    qi = pl.program_id(1)
    m_ref[...] = jnp.full(m_ref.shape, NEG_INF, F32)
    acc_ref[...] = jnp.zeros(acc_ref.shape, F32)
    row = qi * tq + lax.broadcasted_iota(jnp.int32, (tq, LANES), 0)
    col = lax.broadcasted_iota(jnp.int32, (tq, LANES), 1)
    ones = jnp.ones((tk, LANES), BF16)
    nslab = tk // LANES

    def score_slabs(hh, start, masked):
        qh = q_ref[:, hh * QHEAD:(hh + 1) * QHEAD]
        kj = k_ref[pl.ds(start, tk), hh * QHEAD:(hh + 1) * QHEAD]
        s = _dot_nt(qh, kj)
        slabs = [s[:, i * LANES:(i + 1) * LANES] for i in range(nslab)]
        if masked:
            slabs = [jnp.where(start + i * LANES + col <= row, sl, NEG_INF)
                     for i, sl in enumerate(slabs)]
        return slabs

    def block(j, masked, st):
        start = pl.multiple_of(j * tk, tk)
        for hh in range(MLA_HEADS):
            slabs = score_slabs(hh, start, masked)
            m_old = m_ref[st, hh]
            m_new = jnp.maximum(m_old, jnp.max(functools.reduce(jnp.maximum, slabs), axis=-1,
                                               keepdims=True))
            alpha = jnp.exp2(m_old - m_new)
            p = jnp.concatenate([jnp.exp2(sl - m_new) for sl in slabs], axis=1).astype(BF16)
            vj = v_ref[pl.ds(start, tk), hh * MLA_DV:(hh + 1) * MLA_DV]
            pv = _dot(p, jnp.concatenate([vj, ones], axis=1))
            acc_ref[st, hh] = acc_ref[st, hh] * jnp.concatenate([alpha, alpha], axis=1) + pv
            m_ref[st, hh] = m_new

    def body(i, carry):
        block(2 * i, False, 0)
        block(2 * i + 1, False, 1)
        return carry

    lax.fori_loop(0, qi // 2, body, 0)

    @pl.when(qi % 2 == 1)
    def _():
        block(qi - 1, False, 0)

    block(qi, True, 1)
    for hh in range(MLA_HEADS):
        m0, m1 = m_ref[0, hh], m_ref[1, hh]
        m = jnp.maximum(m0, m1)
        w0, w1 = jnp.exp2(m0 - m), jnp.exp2(m1 - m)
        acc = (acc_ref[0, hh] * jnp.concatenate([w0, w0], axis=1)
               + acc_ref[1, hh] * jnp.concatenate([w1, w1], axis=1))
        gate = g_ref[:, hh * MLA_DV:(hh + 1) * MLA_DV].astype(F32)
        o_ref[:, hh * MLA_DV:(hh + 1) * MLA_DV] = (acc[:, :MLA_DV] / acc[:, MLA_DV:]
                                                   * gate).astype(o_ref.dtype)


def _attn_prompt(q, kcat, v, ga, *, batch, seq, tq, tk):
    assert tq == tk, "one diagonal key block per query tile"
    nq = seq // tq
    return pl.pallas_call(
        functools.partial(_attn_prompt_kernel, tq, tk),
        grid=(batch, nq),
        in_specs=[pl.BlockSpec((tq, QCOLS), lambda b, i: (b * nq + i, 0)),
                  pl.BlockSpec((seq, QCOLS), lambda b, i: (b, 0)),
                  pl.BlockSpec((seq, MLA_WIDTH), lambda b, i: (b, 0)),
                  pl.BlockSpec((tq, MLA_WIDTH), lambda b, i: (b * nq + i, 0))],
        out_specs=pl.BlockSpec((tq, MLA_WIDTH), lambda b, i: (b * nq + i, 0)),
        out_shape=jax.ShapeDtypeStruct((q.shape[0], MLA_WIDTH), BF16),
        scratch_shapes=[pltpu.VMEM((2, MLA_HEADS, tq, LANES), F32),
                        pltpu.VMEM((2, MLA_HEADS, tq, 2 * MLA_DV), F32)],
        compiler_params=pltpu.CompilerParams(dimension_semantics=("arbitrary", "arbitrary"),
                                             vmem_limit_bytes=VMEM_LIMIT),
        name="attn_prompt",
    )(q, kcat, v, ga)


NBUF = 3


def _col_from_row(row_vec, eye):
    return jnp.sum(jnp.where(eye, jnp.broadcast_to(row_vec, eye.shape), 0.0), axis=1,
                   keepdims=True)


def _page_copies(layer, ppb, pt_ref, ckv_hbm, kpet_hbm, cbuf, kpbuf, sems, step, slot):
    copies = []
    for j in range(ppb):
        page = pt_ref[step * ppb + j]
        rows = pl.ds(j * PAGE_SIZE, PAGE_SIZE)
        copies.append(pltpu.make_async_copy(ckv_hbm.at[layer, page], cbuf.at[slot, rows],
                                            sems.at[slot, 0]))
        copies.append(pltpu.make_async_copy(kpet_hbm.at[layer, page],
                                            kpbuf.at[slot, pl.ds(0, MLA_ROPE), rows],
                                            sems.at[slot, 1]))
    return copies


def _start_all(copies):
    for i, cp in enumerate(copies):
        cp.start(priority=(i // 2) % 2)


def _attn_sample_kernel(layer, ppb, t, nblk, pt_ref, q_ref, cnew_ref, kpnew_ref, g_ref, wkbp_ref,
                        wkb_ref, wvb_ref, knw_ref, ones_ref, ckv_hbm, kpet_hbm, o_ref,
                        cbuf, kpbuf, sems, cb_ref, sn_ref, r2_ref, m_ref, l_ref, acc_ref, qtil_ref,
                        b2t_ref):
    b = pl.program_id(0)
    nseq = pl.num_programs(0) - 1
    nstep = nseq * nblk
    group = MLA_HEADS * t
    nsub = LANES // group
    rows = cbuf.shape[1]
    sub = rows // nsub
    copies = functools.partial(_page_copies, layer, ppb, pt_ref, ckv_hbm, kpet_hbm, cbuf, kpbuf,
                               sems)

    ri = lax.broadcasted_iota(jnp.int32, (LANES, LANES), 0)
    ci = lax.broadcasted_iota(jnp.int32, (LANES, LANES), 1)
    eye = ri == ci

    def build_operands():
        q = q_ref[...]
        qn = jnp.concatenate([q[:, hh * QHEAD:hh * QHEAD + LANES] for hh in range(MLA_HEADS)],
                             axis=1) * jnp.concatenate([knw_ref[...]] * MLA_HEADS, axis=1)
        qn = jnp.concatenate([qn] * (LANES // t), axis=0)
        rh = (lax.broadcasted_iota(jnp.int32, qn.shape, 0) % group) // t
        lh = lax.broadcasted_iota(jnp.int32, qn.shape, 1) // MLA_NOPE
        qt = jnp.where(rh == lh, qn, 0.0).astype(BF16)
        qtil = _dot_nt(wkb_ref[...], qt)
        lane_g = lax.broadcasted_iota(jnp.int32, qtil.shape, 1) // group
        qp = jnp.concatenate([q[:, hh * QHEAD + LANES:(hh + 1) * QHEAD] for hh in range(MLA_HEADS)],
                             axis=0)
        qp = jnp.concatenate([qp] * nsub, axis=0)
        row_g = lax.broadcasted_iota(jnp.int32, qp.shape, 0) // group
        for g in range(nsub):
            qtil_ref[g] = jnp.where(lane_g == g, qtil, 0.0).astype(BF16)
            b2t_ref[g, LANES:, :] = jnp.concatenate(
                [jnp.zeros(qp.shape, F32), jnp.where(row_g == g, qp, 0.0)], axis=1).astype(BF16)

    @pl.when(b == 0)
    def _():
        kpbuf[:, MLA_ROPE:, :] = jnp.zeros((NBUF, LANES - MLA_ROPE, rows), F32)
        for g in range(nsub):
            b2t_ref[g, :LANES, :] = ones_ref[g]
        last = (nblk - 1) % 2
        sn_ref[last] = jnp.zeros(sn_ref.shape[1:], F32)
        r2_ref[last] = jnp.zeros(r2_ref.shape[1:], F32)
        cb_ref[last] = jnp.zeros(cb_ref.shape[1:], BF16)
        m_ref[...] = jnp.full(m_ref.shape, NEG_INF, F32)
        l_ref[...] = jnp.zeros(l_ref.shape, F32)
        acc_ref[...] = jnp.zeros(acc_ref.shape, F32)
        for blk in range(NBUF - 1):
            _start_all(copies(blk, blk))

    def scores(c_bf, kp_pad, g):
        kf = _dot(c_bf, wkbp_ref[...])
        sq = kf * kf
        fold = (sq[:, 0:LANES] + sq[:, LANES:2 * LANES]) + (sq[:, 2 * LANES:3 * LANES]
                                                            + sq[:, 3 * LANES:4 * LANES])
        lhs2 = jnp.concatenate([fold.astype(BF16), kp_pad.astype(BF16)], axis=1)
        return _dot(c_bf, qtil_ref[g]), _dot_nt(lhs2, b2t_ref[g])

    def softmax_step(sn, r2, valid):
        s = sn * lax.rsqrt(r2[:, :LANES] * (1.0 / MLA_NOPE) + EPS) + r2[:, LANES:]
        if valid is not None:
            s = jnp.where(valid, s, NEG_INF)
        m_old = m_ref[0:1, :]
        m_new = jnp.maximum(m_old, jnp.max(s, axis=0, keepdims=True))
        alpha = jnp.exp(m_old - m_new)
        p = jnp.exp(s - m_new)
        if valid is not None:
            p = jnp.where(valid, p, 0.0)
        l_ref[0:1, :] = l_ref[0:1, :] * alpha + jnp.sum(p, axis=0, keepdims=True)
        m_ref[0:1, :] = m_new
        return p, _col_from_row(alpha, eye)

    def finish_sequence():
        m_col = _col_from_row(m_ref[0:1, :], eye)
        l_col = _col_from_row(l_ref[0:1, :], eye)
        parts = [slice(g * group, (g + 1) * group) for g in range(nsub)]
        m_fin = functools.reduce(jnp.maximum, [m_col[sl] for sl in parts])
        w = [jnp.exp(m_col[sl] - m_fin) for sl in parts]
        l_fin = sum(w[g] * l_col[parts[g]] for g in range(nsub))
        lat = sum(w[g] * acc_ref[parts[g], :] for g in range(nsub)) / l_fin
        full = _dot(lat.astype(BF16), wvb_ref[...])
        o = jnp.concatenate([full[hh * t:(hh + 1) * t, hh * MLA_DV:(hh + 1) * MLA_DV]
                             for hh in range(MLA_HEADS)], axis=1)
        o_ref[...] = (o * g_ref[...]).astype(o_ref.dtype)

    for k in range(nblk):
        v = b * nblk + k
        cur, prv = k % 2, (k + 1) % 2
        ring = v % NBUF
        for cp in copies(jnp.minimum(v, nstep - 1), ring):
            cp.wait()
        _start_all(copies(jnp.minimum(v + NBUF - 1, nstep - 1), (v + NBUF - 1) % NBUF))
        if k == 0:
            build_operands()

        p, alpha_col = softmax_step(sn_ref[prv], r2_ref[prv], None)
        lane_g = lax.broadcasted_iota(jnp.int32, p.shape, 1) // group
        p_rows = jnp.concatenate([jnp.where(lane_g == g, p, 0.0) for g in range(nsub)], axis=0)
        acc_ref[...] = acc_ref[...] * alpha_col + _dot_tn(p_rows.astype(BF16), cb_ref[prv])

        sn = r2 = None
        for g in range(nsub):
            rs = pl.ds(g * sub, sub)
            c_g = cbuf[ring, rs, :].astype(BF16)
            cb_ref[cur, rs, :] = c_g
            sn_g, r2_g = scores(c_g, kpbuf[ring, :, rs].T, g)
            sn = sn_g if sn is None else sn + sn_g
            r2 = r2_g if r2 is None else r2 + r2_g
        sn_ref[cur] = sn
        r2_ref[cur] = r2

        if k == 0:
            pad = 2 * SUBLANES - t
            c_new = jnp.concatenate([cnew_ref[...], jnp.zeros((pad, KV_RANK), F32)],
                                    axis=0).astype(BF16)
            kp_new = jnp.concatenate([kpnew_ref[...], jnp.zeros((pad, LANES), F32)], axis=0)
            sn0, r20 = scores(c_new, kp_new, 0)
            pl.when(b > 0)(finish_sequence)
            m_ref[...] = jnp.full(m_ref.shape, NEG_INF, F32)
            l_ref[...] = jnp.zeros(l_ref.shape, F32)
            r = lax.broadcasted_iota(jnp.int32, (2 * SUBLANES, LANES), 0)
            j = lax.broadcasted_iota(jnp.int32, (2 * SUBLANES, LANES), 1)
            p0, _ = softmax_step(sn0, r20, (r < t) & (r <= j % t) & (j < group))
            acc_ref[...] = _dot_tn(p0.astype(BF16), c_new)

    @pl.when(b == nseq)
    def _():
        v_last = b * nblk + nblk - 1
        for k in range(1, NBUF):
            for cp in copies(nstep - 1, (v_last + k) % NBUF):
                cp.wait()


def _attn_sample(page_table, q, c_new, kp_new, ga, w_kb, w_vb, kn_w, cache_ckv, cache_kpet, *,
                 layer, t, ppb):
    nseq, npages = page_table.shape
    nblk = npages // ppb
    rows = ppb * PAGE_SIZE
    group = MLA_HEADS * t
    nsub = LANES // group
    fold = MLA_NOPE // nsub
    w_kbp = w_kb.reshape(KV_RANK, MLA_HEADS, nsub, fold).transpose(0, 2, 1, 3).reshape(w_kb.shape)
    j = np.arange(LANES)
    d = np.arange(2 * LANES)
    ones_top = ((d[None, None, :] < LANES)
                & (d[None, None, :] // fold == ((j % group) // t)[None, :, None])
                & (j[None, :, None] // group == np.arange(nsub)[:, None, None]))
    ones_top = jnp.asarray(ones_top, BF16)
    assert nblk % 2 == 0 and nseq * nblk >= NBUF, "scratch parity is static per block of a sequence"
    seq_now = lambda b, pt: (jnp.minimum(b, nseq - 1), 0)
    seq_prev = lambda b, pt: (jnp.maximum(b - 1, 0), 0)
    const = lambda b, pt: (0, 0)
    sub = rows // nsub
    grid_spec = pltpu.PrefetchScalarGridSpec(
        num_scalar_prefetch=1,
        grid=(nseq + 1,),
        in_specs=[pl.BlockSpec((t, QCOLS), seq_now),
                  pl.BlockSpec((t, KV_RANK), seq_now),
                  pl.BlockSpec((t, LANES), seq_now),
                  pl.BlockSpec((t, MLA_WIDTH), seq_prev),
                  pl.BlockSpec(w_kbp.shape, const),
                  pl.BlockSpec(w_kb.shape, const),
                  pl.BlockSpec(w_vb.shape, const),
                  pl.BlockSpec(kn_w.shape, const),
                  pl.BlockSpec(ones_top.shape, lambda s, pt: (0, 0, 0)),
                  pl.BlockSpec(memory_space=pl.ANY),
                  pl.BlockSpec(memory_space=pl.ANY)],
        out_specs=pl.BlockSpec((t, MLA_WIDTH), seq_prev),
        scratch_shapes=[pltpu.VMEM((NBUF, rows, KV_RANK), F32),
                        pltpu.VMEM((NBUF, LANES, rows), F32),
                        pltpu.SemaphoreType.DMA((NBUF, 2)),
                        pltpu.VMEM((2, rows, KV_RANK), BF16),
                        pltpu.VMEM((2, sub, LANES), F32),
                        pltpu.VMEM((2, sub, 2 * LANES), F32),
                        pltpu.VMEM((SUBLANES, LANES), F32),
                        pltpu.VMEM((SUBLANES, LANES), F32),
                        pltpu.VMEM((LANES, KV_RANK), F32),
                        pltpu.VMEM((nsub, KV_RANK, LANES), BF16),
                        pltpu.VMEM((nsub, 2 * LANES, 2 * LANES), BF16)])
    return pl.pallas_call(
        functools.partial(_attn_sample_kernel, layer, ppb, t, nblk),
        grid_spec=grid_spec,
        out_shape=jax.ShapeDtypeStruct((nseq * t, MLA_WIDTH), F32),
        compiler_params=pltpu.CompilerParams(dimension_semantics=("arbitrary",),
                                             vmem_limit_bytes=VMEM_LIMIT),
        name="attn_sample",
    )(page_table.reshape(-1), q, c_new, kp_new, ga, w_kbp, w_kb, w_vb, kn_w, ones_top, cache_ckv,
      cache_kpet)


def _outproj_kernel(or_ref, oa_ref, x_ref, w_ref, y_ref):
    y = _dot(or_ref[...].astype(BF16), w_ref[:RET_WIDTH, :])
    y = y + _dot(oa_ref[...].astype(BF16), w_ref[RET_WIDTH:, :])
    y_ref[...] = x_ref[...] + y


def _outproj(o_r, o_a, x, w_out, *, tm):
    n = x.shape[0]
    row = lambda i: (i, 0)
    return pl.pallas_call(
        _outproj_kernel,
        grid=(n // tm,),
        in_specs=[pl.BlockSpec((tm, RET_WIDTH), row), pl.BlockSpec((tm, MLA_WIDTH), row),
                  pl.BlockSpec((tm, D_MODEL), row), pl.BlockSpec(w_out.shape, lambda i: (0, 0))],
        out_specs=pl.BlockSpec((tm, D_MODEL), row),
        out_shape=jax.ShapeDtypeStruct(x.shape, F32),
        compiler_params=pltpu.CompilerParams(dimension_semantics=("arbitrary",),
                                             vmem_limit_bytes=VMEM_LIMIT),
        name="outproj",
    )(o_r, o_a, x, w_out)


def _rope_tables(pos):
    def cs(dim):
        inv = ROPE_BASE ** (-(jnp.arange(0, dim, 2, dtype=F32) / dim))
        ang = pos.astype(F32)[:, None] * inv[None, :]
        return jnp.cos(ang), jnp.sin(ang)

    cr, sr = cs(RET_DK)
    ca, sa = cs(MLA_ROPE)
    zeros = jnp.zeros((pos.shape[0], LANES - MLA_ROPE), F32)
    return (jnp.concatenate([cr, cr], axis=1), jnp.concatenate([-sr, sr], axis=1),
            jnp.concatenate([ca, ca, zeros], axis=1), jnp.concatenate([-sa, sa, zeros], axis=1))


def _decay_tables(chunk, nseq):
    log_g = jnp.log1p(-jnp.exp2(-5.0 - jnp.arange(RET_HEADS, dtype=F32)))
    i = jnp.arange(chunk, dtype=F32)
    diff = i[:, None] - i[None, :]
    intra = jnp.where(diff >= 0, jnp.exp(log_g[:, None, None] * jnp.maximum(diff, 0.0)), 0.0)
    if nseq > 1:
        same = jnp.eye(nseq, dtype=F32)
        intra = jnp.einsum("ab,hij->haibj", same, intra).reshape(RET_HEADS, nseq * chunk,
                                                                 nseq * chunk)
    q_dec = jnp.exp(log_g[None, :] * (i[:, None] + 1.0))
    k_dec = jnp.exp(log_g[None, :] * (chunk - 1.0 - i[:, None]))
    widen = lambda a: jnp.tile(jnp.repeat(a, RET_DK, axis=1), (nseq, 1))
    g = 1.0 - np.exp2(-5.0 - np.arange(RET_HEADS, dtype=np.float64))
    sdec = tuple(float(v) for v in np.exp(np.log(g) * chunk))
    return {"intra": intra, "qdec": widen(q_dec), "kdec": widen(k_dec), "sdec": sdec}


def _prep_weights(norm_w, w_in, q_a_norm_w, w_qb, qn_w, qp_w, kv_a_norm_w, kp_w, w_kb, kn_w, w_vb,
                  ret_gn_w, w_out):
    depth = w_in.shape[0]
    s = np.cumsum((512, 512, 512, 512, Q_RANK, KV_RANK, MLA_ROPE, MLA_WIDTH))
    w_in_r = jnp.pad(w_in[:, :, :s[6]], ((0, 0), (0, 0), (0, LANES - MLA_ROPE))).astype(BF16)
    w_ga = w_in[:, :, s[6]:].astype(BF16)
    wq = w_qb.reshape(depth, Q_RANK, MLA_HEADS, MLA_NOPE + MLA_ROPE)
    wq = jnp.pad(wq, ((0, 0), (0, 0), (0, 0), (0, QHEAD - MLA_NOPE - MLA_ROPE)))
    wq = wq.reshape(depth, Q_RANK, QCOLS).astype(BF16)
    pad_rope = lambda w: jnp.pad(w, ((0, 0), (0, LANES - MLA_ROPE)))
    layers = []
    for l in range(depth):
        layers.append({
            "norm_w": norm_w[l][None], "w_in": w_in_r[l], "w_ga": w_ga[l], "q_a_norm_w": q_a_norm_w[l][None],
            "w_qb": wq[l], "qn_w": qn_w[l][None], "qp_w": pad_rope(qp_w)[l][None],
            "kv_a_norm_w": kv_a_norm_w[l][None], "kp_w": pad_rope(kp_w)[l][None],
            "w_kb": w_kb[l].astype(BF16), "kn_w": kn_w[l][None], "w_vb": w_vb[l].astype(BF16),
            "ret_gn_w": ret_gn_w[l][None], "w_out": w_out[l].astype(BF16)})
    return layers


def kernel(x_prompt, x_sample, cache_ckv, cache_kpe, state_ret, page_table, norm_w, w_in, q_a_norm_w,
           w_qb, qn_w, qp_w, kv_a_norm_w, kp_w, w_kb, kn_w, w_vb, ret_gn_w, w_out):
    b_p, t_p, _ = x_prompt.shape
    b_s, t_s, _ = x_sample.shape
    depth = w_in.shape[0]
    past = page_table.shape[1] * PAGE_SIZE
    tm_p = 512
    tm_s = 512
    ret_nseq = 8
    ret_bb = 4
    ppb = 32
    cache_kpet = jnp.swapaxes(cache_kpe, 2, 3)

    layers = _prep_weights(norm_w, w_in, q_a_norm_w, w_qb, qn_w, qp_w, kv_a_norm_w, kp_w, w_kb, kn_w,
                           w_vb, ret_gn_w, w_out)
    tabs_p = _rope_tables(jnp.arange(t_p, dtype=jnp.int32))
    tabs_s = _rope_tables(past + jnp.arange(t_s, dtype=jnp.int32))
    tabs_s = tuple(jnp.tile(a, (tm_s // t_s, 1)) for a in tabs_s)
    dec_p = _decay_tables(RET_CHUNK, 1)
    dec_s = _decay_tables(t_s, ret_nseq)

    hp = x_prompt.reshape(b_p * t_p, D_MODEL)
    hs = x_sample.reshape(b_s * t_s, D_MODEL)
    outs = {k: [] for k in ("ckv_p", "kpe_p", "ret_p", "ckv_s", "kpe_s", "ret_s")}
    for l in range(depth):
        lw = layers[l]
        qr, kr, vr, gr, ga, q, c, kpe, kcat, v = _front(hp, tabs_p, lw, tm=tm_p, emit_kv=True,
                                                        act_dtype=BF16)
        o_r, s_new = _ret_prompt(qr, kr, vr, gr, dec_p, lw["ret_gn_w"], batch=b_p, seq=t_p,
                                 bb=ret_bb)
        o_a = _attn_prompt(q, kcat, v, ga, batch=b_p, seq=t_p, tq=256, tk=256)
        hp = _outproj(o_r, o_a, hp, lw["w_out"], tm=tm_p)
        outs["ckv_p"].append(c.reshape(b_p, t_p, KV_RANK))
        outs["kpe_p"].append(kpe.reshape(b_p, t_p, MLA_ROPE))
        outs["ret_p"].append(s_new)

        qr, kr, vr, gr, ga, q, c, kpe, kp128 = _front(hs, tabs_s, lw, tm=tm_s, emit_kv=False,
                                                      act_dtype=F32)
        o_r, s_new = _ret_sample(qr, kr, vr, gr, state_ret, dec_s, lw["ret_gn_w"], layer=l, t=t_s,
                                 nseq=ret_nseq)
        o_a = _attn_sample(page_table, q, c, kp128, ga, lw["w_kb"], lw["w_vb"], lw["kn_w"],
                           cache_ckv, cache_kpet, layer=l, t=t_s, ppb=ppb)
        hs = _outproj(o_r, o_a, hs, lw["w_out"], tm=tm_s)
        outs["ckv_s"].append(c.reshape(b_s, t_s, KV_RANK))
        outs["kpe_s"].append(kpe.reshape(b_s, t_s, MLA_ROPE))
        outs["ret_s"].append(s_new)

    return (hp.reshape(b_p, t_p, D_MODEL), hs.reshape(b_s, t_s, D_MODEL),
            jnp.stack(outs["ckv_p"]), jnp.stack(outs["kpe_p"]), jnp.stack(outs["ret_p"]),
            jnp.stack(outs["ckv_s"]), jnp.stack(outs["kpe_s"]), jnp.stack(outs["ret_s"]))
```

```python
import functools

import numpy as np
import jax
import jax.numpy as jnp
from jax import lax
from jax.experimental import pallas as pl
from jax.experimental.pallas import tpu as pltpu

F32 = jnp.float32
BF16 = jnp.bfloat16

D_MODEL = 1024
PAGE_SIZE = 128
RET_HEADS = 4
RET_DK = 128
RET_DV = 128
RET_WIDTH = RET_HEADS * RET_DV
RET_CHUNK = 256
MLA_HEADS = 4
MLA_NOPE = 128
MLA_ROPE = 64
MLA_DV = 128
MLA_WIDTH = MLA_HEADS * MLA_DV
Q_RANK = 384
KV_RANK = 256
MLA_SCALE = (MLA_NOPE + MLA_ROPE) ** -0.5
LOG2E = 1.4426950408889634
ROPE_BASE = 10000.0
EPS = 1e-6
GN_EPS = 1e-5
NEG_INF = -1e30

LANES = 128
SUBLANES = 8
VMEM_LIMIT = 56 * 1024 * 1024

OFF_QR = 0
OFF_KR = 512
OFF_VR = 1024
OFF_GR = 1536
OFF_QL = 2048
OFF_C = OFF_QL + Q_RANK
OFF_KPE = OFF_C + KV_RANK
IN_COLS_PAD = OFF_KPE + LANES
QHEAD = 2 * LANES
QCOLS = MLA_HEADS * QHEAD

NT_DIMS = (((1,), (1,)), ((), ()))
TN_DIMS = (((0,), (0,)), ((), ()))


def _dot(a, b):
    return jnp.dot(a, b, preferred_element_type=F32)


def _dot_nt(a, b):
    return lax.dot_general(a, b, NT_DIMS, preferred_element_type=F32)


def _dot_tn(a, b):
    return lax.dot_general(a, b, TN_DIMS, preferred_element_type=F32)


def _rms_scale(v, n):
    return lax.rsqrt(jnp.sum(v * v, axis=-1, keepdims=True) * (1.0 / n) + EPS)


def _rope64(u, cos, sin, lane):
    swapped = jnp.where(lane < MLA_ROPE // 2,
                        pltpu.roll(u, LANES - MLA_ROPE // 2, 1),
                        pltpu.roll(u, MLA_ROPE // 2, 1))
    return u * cos + swapped * sin


def _front_kernel(emit_kv, qscale, x_ref, cosr_ref, sinr_ref, cosa_ref, sina_ref, normw_ref, win_ref, wga_ref,
                  qanw_ref, wqb_ref, qnw_ref, qpw_ref, kvnw_ref, kpw_ref, wkb_ref, knw_ref,
                  wvb_ref, qr_ref, kr_ref, vr_ref, gr_ref, ga_ref, q_ref, c_ref, kpe_ref,
                  *kv_refs):
    x = x_ref[...]
    h = (x * _rms_scale(x, D_MODEL) * normw_ref[...]).astype(BF16)
    cosr, sinr = cosr_ref[...], sinr_ref[...]
    cosa, sina = cosa_ref[...], sina_ref[...]
    lane = lax.broadcasted_iota(jnp.int32, cosa.shape, 1)

    half = 2 * RET_DK
    for c0 in range(0, RET_WIDTH, half):
        cs = slice(c0, c0 + half)
        zq = _dot(h, win_ref[:, OFF_QR + c0:OFF_QR + c0 + half])
        zk = _dot(h, win_ref[:, OFF_KR + c0:OFF_KR + c0 + half])
        for d0 in range(0, half, RET_DK):
            sl = slice(c0 + d0, c0 + d0 + RET_DK)
            qh, kh = zq[:, d0:d0 + RET_DK], zk[:, d0:d0 + RET_DK]
            qr_ref[:, sl] = (qh * cosr + pltpu.roll(qh, RET_DK // 2, 1) * sinr).astype(qr_ref.dtype)
            kr_ref[:, sl] = ((kh * cosr + pltpu.roll(kh, RET_DK // 2, 1) * sinr)
                             * (RET_DK ** -0.5)).astype(kr_ref.dtype)
        vr_ref[:, cs] = _dot(h, win_ref[:, OFF_VR + c0:OFF_VR + c0 + half]).astype(vr_ref.dtype)
        zg = _dot(h, win_ref[:, OFF_GR + c0:OFF_GR + c0 + half])
        gr_ref[:, cs] = (zg * jax.nn.sigmoid(zg)).astype(gr_ref.dtype)
        zg = _dot(h, wga_ref[:, cs])
        ga_ref[:, cs] = (zg * jax.nn.sigmoid(zg)).astype(ga_ref.dtype)

    zql = _dot(h, win_ref[:, OFF_QL:OFF_QL + Q_RANK])
    qa = (zql * _rms_scale(zql, Q_RANK) * qanw_ref[...]).astype(BF16)
    q = _dot(qa, wqb_ref[...])
    for hh in range(MLA_HEADS):
        base = hh * QHEAD
        qn = q[:, base:base + LANES]
        qn = qn * _rms_scale(qn, MLA_NOPE) * qnw_ref[...] * qscale
        qp = q[:, base + LANES:base + QHEAD]
        u = qp * _rms_scale(qp, MLA_ROPE) * qpw_ref[...]
        qp = _rope64(u, cosa, sina, lane) * qscale
        q_ref[:, base:base + LANES] = qn.astype(q_ref.dtype)
        q_ref[:, base + LANES:base + QHEAD] = qp.astype(q_ref.dtype)

    zck = _dot(h, win_ref[:, OFF_C:OFF_C + KV_RANK + LANES])
    zc = zck[:, :KV_RANK]
    c = zc * _rms_scale(zc, KV_RANK) * kvnw_ref[...]
    c_ref[...] = c
    zp = zck[:, KV_RANK:]
    u = zp * _rms_scale(zp, MLA_ROPE) * kpw_ref[...]
    kp = _rope64(u, cosa, sina, lane)
    kpe_ref[...] = kp[:, :MLA_ROPE]

    if emit_kv:
        kcat_ref, v_ref = kv_refs
        cb = c.astype(BF16)
        kf = _dot(cb, wkb_ref[...])
        for hh in range(MLA_HEADS):
            kn = kf[:, hh * MLA_NOPE:(hh + 1) * MLA_NOPE]
            kn = kn * _rms_scale(kn, MLA_NOPE) * knw_ref[...]
            kcat_ref[:, hh * QHEAD:hh * QHEAD + LANES] = kn.astype(kcat_ref.dtype)
            kcat_ref[:, hh * QHEAD + LANES:(hh + 1) * QHEAD] = kp.astype(kcat_ref.dtype)
        v_ref[...] = _dot(cb, wvb_ref[...]).astype(v_ref.dtype)
    else:
        (kp128_ref,) = kv_refs
        kp128_ref[...] = kp


def _front(x, tabs, lw, *, tm, emit_kv, act_dtype):
    n = x.shape[0]
    tper = tabs[0].shape[0]
    nper = tper // tm
    row = lambda i: (i, 0)
    const = lambda i: (0, 0)
    tab_spec = pl.BlockSpec((tm, LANES), lambda i: (i % nper, 0))

    def full(a):
        return pl.BlockSpec(a.shape, const)

    weights = [lw["norm_w"], lw["w_in"], lw["w_ga"], lw["q_a_norm_w"], lw["w_qb"], lw["qn_w"], lw["qp_w"],
               lw["kv_a_norm_w"], lw["kp_w"], lw["w_kb"], lw["kn_w"], lw["w_vb"]]
    in_specs = [pl.BlockSpec((tm, D_MODEL), row)] + [tab_spec] * 4 + [full(w) for w in weights]
    out_shape = [jax.ShapeDtypeStruct((n, 512), BF16)] * 3
    out_shape += [jax.ShapeDtypeStruct((n, 512), BF16)]
    out_shape += [jax.ShapeDtypeStruct((n, 512), act_dtype)]
    out_shape += [jax.ShapeDtypeStruct((n, QCOLS), act_dtype)]
    out_shape += [jax.ShapeDtypeStruct((n, KV_RANK), F32)]
    out_shape += [jax.ShapeDtypeStruct((n, MLA_ROPE), F32)]
    if emit_kv:
        out_shape += [jax.ShapeDtypeStruct((n, QCOLS), BF16)]
        out_shape += [jax.ShapeDtypeStruct((n, MLA_WIDTH), BF16)]
    else:
        out_shape += [jax.ShapeDtypeStruct((n, LANES), F32)]
    out_specs = [pl.BlockSpec((tm, s.shape[1]), row) for s in out_shape]
    return pl.pallas_call(
        functools.partial(_front_kernel, emit_kv, MLA_SCALE * LOG2E if emit_kv else MLA_SCALE),
        grid=(n // tm,),
        in_specs=in_specs,
        out_specs=out_specs,
        out_shape=out_shape,
        compiler_params=pltpu.CompilerParams(dimension_semantics=("arbitrary",),
                                             vmem_limit_bytes=VMEM_LIMIT),
        name="front_kv" if emit_kv else "front",
    )(x, *tabs, *weights)


def _ret_prompt_kernel(sdec, q_ref, k_ref, v_ref, g_ref, intra_ref, qdec_ref, kdec_ref, gnw_ref,
                       o_ref, sout_ref, s_ref):
    c = pl.program_id(1)

    @pl.when(c == 0)
    def _():
        s_ref[...] = jnp.zeros_like(s_ref)

    for b in range(q_ref.shape[0]):
        for hh in range(RET_HEADS):
            sl = slice(hh * RET_DK, (hh + 1) * RET_DK)
            q, k, v = q_ref[b, :, sl], k_ref[b, :, sl], v_ref[b, :, sl]
            s = s_ref[b, hh]
            a = _dot_nt(q, k) * intra_ref[hh]
            o = _dot(a.astype(BF16), v) + _dot(q, s.astype(BF16)) * qdec_ref[:, sl]
            kd = (k.astype(F32) * kdec_ref[:, sl]).astype(BF16)
            s_ref[b, hh] = s * sdec[hh] + _dot_tn(kd, v)
            mu = jnp.mean(o, axis=-1, keepdims=True)
            d = o - mu
            var = jnp.mean(d * d, axis=-1, keepdims=True)
            on = d * lax.rsqrt(var + GN_EPS) * gnw_ref[:, sl]
            o_ref[b, :, sl] = (on * g_ref[b, :, sl].astype(F32)).astype(o_ref.dtype)

    @pl.when(c == pl.num_programs(1) - 1)
    def _():
        sout_ref[...] = s_ref[...]


def _ret_prompt(qr, kr, vr, gr, dec, gnw, *, batch, seq, bb):
    nc = seq // RET_CHUNK
    as3d = lambda a: a.reshape(batch, seq, RET_WIDTH)
    blk = pl.BlockSpec((bb, RET_CHUNK, RET_WIDTH), lambda b, c: (b, c, 0))
    const2 = lambda b, c: (0, 0)
    o, s = pl.pallas_call(
        functools.partial(_ret_prompt_kernel, dec["sdec"]),
        grid=(batch // bb, nc),
        in_specs=[blk, blk, blk, blk,
                  pl.BlockSpec(dec["intra"].shape, lambda b, c: (0, 0, 0)),
                  pl.BlockSpec(dec["qdec"].shape, const2),
                  pl.BlockSpec(dec["kdec"].shape, const2),
                  pl.BlockSpec(gnw.shape, const2)],
        out_specs=[blk, pl.BlockSpec((bb, RET_HEADS, RET_DK, RET_DV), lambda b, c: (b, 0, 0, 0))],
        out_shape=[jax.ShapeDtypeStruct((batch, seq, RET_WIDTH), BF16),
                   jax.ShapeDtypeStruct((batch, RET_HEADS, RET_DK, RET_DV), F32)],
        scratch_shapes=[pltpu.VMEM((bb, RET_HEADS, RET_DK, RET_DV), F32)],
        compiler_params=pltpu.CompilerParams(dimension_semantics=("arbitrary", "arbitrary"),
                                             vmem_limit_bytes=VMEM_LIMIT),
        name="ret_prompt",
    )(as3d(qr), as3d(kr), as3d(vr), as3d(gr), dec["intra"], dec["qdec"], dec["kdec"], gnw)
    return o.reshape(batch * seq, RET_WIDTH), s


def _ret_sample_kernel(sdec, t, q_ref, k_ref, v_ref, g_ref, st_ref, intra_ref, qdec_ref, kdec_ref,
                       gnw_ref, o_ref, sout_ref):
    nseq = st_ref.shape[0]
    for hh in range(RET_HEADS):
        sl = slice(hh * RET_DK, (hh + 1) * RET_DK)
        q, k, v = q_ref[:, sl], k_ref[:, sl], v_ref[:, sl]
        a = _dot_nt(q, k) * intra_ref[hh]
        o_intra = _dot(a.astype(BF16), v)
        q32 = q.astype(F32)
        v32 = v.astype(F32)
        kd32 = (k.astype(F32) * kdec_ref[:, sl]).astype(BF16).astype(F32)
        qdec = qdec_ref[:, sl]
        gnw = gnw_ref[:, sl]
        for s in range(nseq):
            rows = slice(s * t, (s + 1) * t)
            s0 = st_ref[s, hh]
            o = o_intra[rows] + _dot(q32[rows], s0) * qdec[rows]
            sout_ref[s, hh] = s0 * sdec[hh] + _dot_tn(kd32[rows], v32[rows])
            mu = jnp.mean(o, axis=-1, keepdims=True)
            d = o - mu
            var = jnp.mean(d * d, axis=-1, keepdims=True)
            on = d * lax.rsqrt(var + GN_EPS) * gnw
            o_ref[rows, sl] = (on * g_ref[rows, sl].astype(F32)).astype(o_ref.dtype)


def _ret_sample(qr, kr, vr, gr, state, dec, gnw, *, layer, t, nseq):
    n = qr.shape[0]
    nb = n // t
    rows = nseq * t
    blk = pl.BlockSpec((rows, RET_WIDTH), lambda i: (i, 0))
    sblk = pl.BlockSpec((nseq, RET_HEADS, RET_DK, RET_DV), lambda i: (i, 0, 0, 0))
    sblk_in = pl.BlockSpec((None, nseq, RET_HEADS, RET_DK, RET_DV), lambda i: (layer, i, 0, 0, 0))
    const2 = lambda i: (0, 0)
    return pl.pallas_call(
        functools.partial(_ret_sample_kernel, dec["sdec"], t),
        grid=(nb // nseq,),
        in_specs=[blk, blk, blk, blk, sblk_in,
                  pl.BlockSpec(dec["intra"].shape, lambda i: (0, 0, 0)),
                  pl.BlockSpec(dec["qdec"].shape, const2),
                  pl.BlockSpec(dec["kdec"].shape, const2),
                  pl.BlockSpec(gnw.shape, const2)],
        out_specs=[blk, sblk],
        out_shape=[jax.ShapeDtypeStruct(qr.shape, BF16),
                   jax.ShapeDtypeStruct(state.shape[1:], F32)],
        compiler_params=pltpu.CompilerParams(dimension_semantics=("arbitrary",),
                                             vmem_limit_bytes=VMEM_LIMIT),
        name="ret_sample",
    )(qr, kr, vr, gr, state, dec["intra"], dec["qdec"], dec["kdec"], gnw)


def _attn_prompt_kernel(tq, tk, q_ref, k_ref, v_ref, g_ref, o_ref, m_ref, acc_ref):
    qi = pl.program_id(1)
    m_ref[...] = jnp.full(m_ref.shape, NEG_INF, F32)
    acc_ref[...] = jnp.zeros(acc_ref.shape, F32)
    row = qi * tq + lax.broadcasted_iota(jnp.int32, (tq, LANES), 0)
    col = lax.broadcasted_iota(jnp.int32, (tq, LANES), 1)
    ones = jnp.ones((tk, LANES), BF16)
    nslab = tk // LANES

    def score_slabs(hh, start, masked):
        qh = q_ref[:, hh * QHEAD:(hh + 1) * QHEAD]
        kj = k_ref[pl.ds(start, tk), hh * QHEAD:(hh + 1) * QHEAD]
        s = _dot_nt(qh, kj)
        slabs = [s[:, i * LANES:(i + 1) * LANES] for i in range(nslab)]
        if masked:
            slabs = [jnp.where(start + i * LANES + col <= row, sl, NEG_INF)
                     for i, sl in enumerate(slabs)]
        return slabs

    def block(j, masked, st):
        start = pl.multiple_of(j * tk, tk)
        for hh in range(MLA_HEADS):
            slabs = score_slabs(hh, start, masked)
            m_old = m_ref[st, hh]
            m_new = jnp.maximum(m_old, jnp.max(functools.reduce(jnp.maximum, slabs), axis=-1,
                                               keepdims=True))
            alpha = jnp.exp2(m_old - m_new)
            p = jnp.concatenate([jnp.exp2(sl - m_new) for sl in slabs], axis=1).astype(BF16)
            vj = v_ref[pl.ds(start, tk), hh * MLA_DV:(hh + 1) * MLA_DV]
            pv = _dot(p, jnp.concatenate([vj, ones], axis=1))
            acc_ref[st, hh] = acc_ref[st, hh] * jnp.concatenate([alpha, alpha], axis=1) + pv
            m_ref[st, hh] = m_new

    def body(i, carry):
        block(2 * i, False, 0)
        block(2 * i + 1, False, 1)
        return carry

    lax.fori_loop(0, qi // 2, body, 0)

    @pl.when(qi % 2 == 1)
    def _():
        block(qi - 1, False, 0)

    block(qi, True, 1)
    for hh in range(MLA_HEADS):
        m0, m1 = m_ref[0, hh], m_ref[1, hh]
        m = jnp.maximum(m0, m1)
        w0, w1 = jnp.exp2(m0 - m), jnp.exp2(m1 - m)
        acc = (acc_ref[0, hh] * jnp.concatenate([w0, w0], axis=1)
               + acc_ref[1, hh] * jnp.concatenate([w1, w1], axis=1))
        gate = g_ref[:, hh * MLA_DV:(hh + 1) * MLA_DV].astype(F32)
        o_ref[:, hh * MLA_DV:(hh + 1) * MLA_DV] = (acc[:, :MLA_DV] / acc[:, MLA_DV:]
                                                   * gate).astype(o_ref.dtype)


def _attn_prompt(q, kcat, v, ga, *, batch, seq, tq, tk):
    assert tq == tk, "one diagonal key block per query tile"
    nq = seq // tq
    return pl.pallas_call(
        functools.partial(_attn_prompt_kernel, tq, tk),
        grid=(batch, nq),
        in_specs=[pl.BlockSpec((tq, QCOLS), lambda b, i: (b * nq + i, 0)),
                  pl.BlockSpec((seq, QCOLS), lambda b, i: (b, 0)),
                  pl.BlockSpec((seq, MLA_WIDTH), lambda b, i: (b, 0)),
                  pl.BlockSpec((tq, MLA_WIDTH), lambda b, i: (b * nq + i, 0))],
        out_specs=pl.BlockSpec((tq, MLA_WIDTH), lambda b, i: (b * nq + i, 0)),
        out_shape=jax.ShapeDtypeStruct((q.shape[0], MLA_WIDTH), BF16),
        scratch_shapes=[pltpu.VMEM((2, MLA_HEADS, tq, LANES), F32),
                        pltpu.VMEM((2, MLA_HEADS, tq, 2 * MLA_DV), F32)],
        compiler_params=pltpu.CompilerParams(dimension_semantics=("arbitrary", "arbitrary"),
                                             vmem_limit_bytes=VMEM_LIMIT),
        name="attn_prompt",
    )(q, kcat, v, ga)


NBUF = 3


def _col_from_row(row_vec, eye):
    return jnp.sum(jnp.where(eye, jnp.broadcast_to(row_vec, eye.shape), 0.0), axis=1,
                   keepdims=True)


def _page_copies(layer, ppb, pt_ref, ckv_hbm, kpet_hbm, cbuf, kpbuf, sems, step, slot):
    copies = []
    for j in range(ppb):
        page = pt_ref[step * ppb + j]
        rows = pl.ds(j * PAGE_SIZE, PAGE_SIZE)
        copies.append(pltpu.make_async_copy(ckv_hbm.at[layer, page], cbuf.at[slot, rows],
                                            sems.at[slot, 0]))
        copies.append(pltpu.make_async_copy(kpet_hbm.at[layer, page],
                                            kpbuf.at[slot, pl.ds(0, MLA_ROPE), rows],
                                            sems.at[slot, 1]))
    return copies


def _start_all(copies):
    for i, cp in enumerate(copies):
        cp.start(priority=(i // 2) % 2)


def _attn_sample_kernel(layer, ppb, t, nblk, pt_ref, q_ref, cnew_ref, kpnew_ref, g_ref, wkbp_ref,
                        wkb_ref, wvb_ref, knw_ref, ones_ref, ckv_hbm, kpet_hbm, o_ref,
                        cbuf, kpbuf, sems, cb_ref, sn_ref, r2_ref, m_ref, l_ref, acc_ref, qtil_ref,
                        b2t_ref):
    b = pl.program_id(0)
    nseq = pl.num_programs(0) - 1
    nstep = nseq * nblk
    group = MLA_HEADS * t
    nsub = LANES // group
    rows = cbuf.shape[1]
    sub = rows // nsub
    copies = functools.partial(_page_copies, layer, ppb, pt_ref, ckv_hbm, kpet_hbm, cbuf, kpbuf,
                               sems)

    ri = lax.broadcasted_iota(jnp.int32, (LANES, LANES), 0)
    ci = lax.broadcasted_iota(jnp.int32, (LANES, LANES), 1)
    eye = ri == ci

    def build_operands():
        q = q_ref[...]
        qn = jnp.concatenate([q[:, hh * QHEAD:hh * QHEAD + LANES] for hh in range(MLA_HEADS)],
                             axis=1) * jnp.concatenate([knw_ref[...]] * MLA_HEADS, axis=1)
        qn = jnp.concatenate([qn] * (LANES // t), axis=0)
        rh = (lax.broadcasted_iota(jnp.int32, qn.shape, 0) % group) // t
        lh = lax.broadcasted_iota(jnp.int32, qn.shape, 1) // MLA_NOPE
        qt = jnp.where(rh == lh, qn, 0.0).astype(BF16)
        qtil = _dot_nt(wkb_ref[...], qt)
        lane_g = lax.broadcasted_iota(jnp.int32, qtil.shape, 1) // group
        qp = jnp.concatenate([q[:, hh * QHEAD + LANES:(hh + 1) * QHEAD] for hh in range(MLA_HEADS)],
                             axis=0)
        qp = jnp.concatenate([qp] * nsub, axis=0)
        row_g = lax.broadcasted_iota(jnp.int32, qp.shape, 0) // group
        for g in range(nsub):
            qtil_ref[g] = jnp.where(lane_g == g, qtil, 0.0).astype(BF16)
            b2t_ref[g, LANES:, :] = jnp.concatenate(
                [jnp.zeros(qp.shape, F32), jnp.where(row_g == g, qp, 0.0)], axis=1).astype(BF16)

    @pl.when(b == 0)
    def _():
        kpbuf[:, MLA_ROPE:, :] = jnp.zeros((NBUF, LANES - MLA_ROPE, rows), F32)
        for g in range(nsub):
            b2t_ref[g, :LANES, :] = ones_ref[g]
        last = (nblk - 1) % 2
        sn_ref[last] = jnp.zeros(sn_ref.shape[1:], F32)
        r2_ref[last] = jnp.zeros(r2_ref.shape[1:], F32)
        cb_ref[last] = jnp.zeros(cb_ref.shape[1:], BF16)
        m_ref[...] = jnp.full(m_ref.shape, NEG_INF, F32)
        l_ref[...] = jnp.zeros(l_ref.shape, F32)
        acc_ref[...] = jnp.zeros(acc_ref.shape, F32)
        for blk in range(NBUF - 1):
            _start_all(copies(blk, blk))

    def scores(c_bf, kp_pad, g):
        kf = _dot(c_bf, wkbp_ref[...])
        sq = kf * kf
        fold = (sq[:, 0:LANES] + sq[:, LANES:2 * LANES]) + (sq[:, 2 * LANES:3 * LANES]
                                                            + sq[:, 3 * LANES:4 * LANES])
        lhs2 = jnp.concatenate([fold.astype(BF16), kp_pad.astype(BF16)], axis=1)
        return _dot(c_bf, qtil_ref[g]), _dot_nt(lhs2, b2t_ref[g])

    def softmax_step(sn, r2, valid):
        s = sn * lax.rsqrt(r2[:, :LANES] * (1.0 / MLA_NOPE) + EPS) + r2[:, LANES:]
        if valid is not None:
            s = jnp.where(valid, s, NEG_INF)
        m_old = m_ref[0:1, :]
        m_new = jnp.maximum(m_old, jnp.max(s, axis=0, keepdims=True))
        alpha = jnp.exp(m_old - m_new)
        p = jnp.exp(s - m_new)
        if valid is not None:
            p = jnp.where(valid, p, 0.0)
        l_ref[0:1, :] = l_ref[0:1, :] * alpha + jnp.sum(p, axis=0, keepdims=True)
        m_ref[0:1, :] = m_new
        return p, _col_from_row(alpha, eye)

    def finish_sequence():
        m_col = _col_from_row(m_ref[0:1, :], eye)
        l_col = _col_from_row(l_ref[0:1, :], eye)
        parts = [slice(g * group, (g + 1) * group) for g in range(nsub)]
        m_fin = functools.reduce(jnp.maximum, [m_col[sl] for sl in parts])
        w = [jnp.exp(m_col[sl] - m_fin) for sl in parts]
        l_fin = sum(w[g] * l_col[parts[g]] for g in range(nsub))
        lat = sum(w[g] * acc_ref[parts[g], :] for g in range(nsub)) / l_fin
        full = _dot(lat.astype(BF16), wvb_ref[...])
        o = jnp.concatenate([full[hh * t:(hh + 1) * t, hh * MLA_DV:(hh + 1) * MLA_DV]
                             for hh in range(MLA_HEADS)], axis=1)
        o_ref[...] = (o * g_ref[...]).astype(o_ref.dtype)

    for k in range(nblk):
        v = b * nblk + k
        cur, prv = k % 2, (k + 1) % 2
        ring = v % NBUF
        for cp in copies(jnp.minimum(v, nstep - 1), ring):
            cp.wait()
        _start_all(copies(jnp.minimum(v + NBUF - 1, nstep - 1), (v + NBUF - 1) % NBUF))
        if k == 0:
            build_operands()

        p, alpha_col = softmax_step(sn_ref[prv], r2_ref[prv], None)
        lane_g = lax.broadcasted_iota(jnp.int32, p.shape, 1) // group
        p_rows = jnp.concatenate([jnp.where(lane_g == g, p, 0.0) for g in range(nsub)], axis=0)
        acc_ref[...] = acc_ref[...] * alpha_col + _dot_tn(p_rows.astype(BF16), cb_ref[prv])

        sn = r2 = None
        for g in range(nsub):
            rs = pl.ds(g * sub, sub)
            c_g = cbuf[ring, rs, :].astype(BF16)
            cb_ref[cur, rs, :] = c_g
            sn_g, r2_g = scores(c_g, kpbuf[ring, :, rs].T, g)
            sn = sn_g if sn is None else sn + sn_g
            r2 = r2_g if r2 is None else r2 + r2_g
        sn_ref[cur] = sn
        r2_ref[cur] = r2

        if k == 0:
            pad = 2 * SUBLANES - t
            c_new = jnp.concatenate([cnew_ref[...], jnp.zeros((pad, KV_RANK), F32)],
                                    axis=0).astype(BF16)
            kp_new = jnp.concatenate([kpnew_ref[...], jnp.zeros((pad, LANES), F32)], axis=0)
            sn0, r20 = scores(c_new, kp_new, 0)
            pl.when(b > 0)(finish_sequence)
            m_ref[...] = jnp.full(m_ref.shape, NEG_INF, F32)
            l_ref[...] = jnp.zeros(l_ref.shape, F32)
            r = lax.broadcasted_iota(jnp.int32, (2 * SUBLANES, LANES), 0)
            j = lax.broadcasted_iota(jnp.int32, (2 * SUBLANES, LANES), 1)
            p0, _ = softmax_step(sn0, r20, (r < t) & (r <= j % t) & (j < group))
            acc_ref[...] = _dot_tn(p0.astype(BF16), c_new)

    @pl.when(b == nseq)
    def _():
        v_last = b * nblk + nblk - 1
        for k in range(1, NBUF):
            for cp in copies(nstep - 1, (v_last + k) % NBUF):
                cp.wait()


def _attn_sample(page_table, q, c_new, kp_new, ga, w_kb, w_vb, kn_w, cache_ckv, cache_kpet, *,
                 layer, t, ppb):
    nseq, npages = page_table.shape
    nblk = npages // ppb
    rows = ppb * PAGE_SIZE
    group = MLA_HEADS * t
    nsub = LANES // group
    fold = MLA_NOPE // nsub
    w_kbp = w_kb.reshape(KV_RANK, MLA_HEADS, nsub, fold).transpose(0, 2, 1, 3).reshape(w_kb.shape)
    j = np.arange(LANES)
    d = np.arange(2 * LANES)
    ones_top = ((d[None, None, :] < LANES)
                & (d[None, None, :] // fold == ((j % group) // t)[None, :, None])
                & (j[None, :, None] // group == np.arange(nsub)[:, None, None]))
    ones_top = jnp.asarray(ones_top, BF16)
    assert nblk % 2 == 0 and nseq * nblk >= NBUF, "scratch parity is static per block of a sequence"
    seq_now = lambda b, pt: (jnp.minimum(b, nseq - 1), 0)
    seq_prev = lambda b, pt: (jnp.maximum(b - 1, 0), 0)
    const = lambda b, pt: (0, 0)
    sub = rows // nsub
    grid_spec = pltpu.PrefetchScalarGridSpec(
        num_scalar_prefetch=1,
        grid=(nseq + 1,),
        in_specs=[pl.BlockSpec((t, QCOLS), seq_now),
                  pl.BlockSpec((t, KV_RANK), seq_now),
                  pl.BlockSpec((t, LANES), seq_now),
                  pl.BlockSpec((t, MLA_WIDTH), seq_prev),
                  pl.BlockSpec(w_kbp.shape, const),
                  pl.BlockSpec(w_kb.shape, const),
                  pl.BlockSpec(w_vb.shape, const),
                  pl.BlockSpec(kn_w.shape, const),
                  pl.BlockSpec(ones_top.shape, lambda s, pt: (0, 0, 0)),
                  pl.BlockSpec(memory_space=pl.ANY),
                  pl.BlockSpec(memory_space=pl.ANY)],
        out_specs=pl.BlockSpec((t, MLA_WIDTH), seq_prev),
        scratch_shapes=[pltpu.VMEM((NBUF, rows, KV_RANK), F32),
                        pltpu.VMEM((NBUF, LANES, rows), F32),
                        pltpu.SemaphoreType.DMA((NBUF, 2)),
                        pltpu.VMEM((2, rows, KV_RANK), BF16),
                        pltpu.VMEM((2, sub, LANES), F32),
                        pltpu.VMEM((2, sub, 2 * LANES), F32),
                        pltpu.VMEM((SUBLANES, LANES), F32),
                        pltpu.VMEM((SUBLANES, LANES), F32),
                        pltpu.VMEM((LANES, KV_RANK), F32),
                        pltpu.VMEM((nsub, KV_RANK, LANES), BF16),
                        pltpu.VMEM((nsub, 2 * LANES, 2 * LANES), BF16)])
    return pl.pallas_call(
        functools.partial(_attn_sample_kernel, layer, ppb, t, nblk),
        grid_spec=grid_spec,
        out_shape=jax.ShapeDtypeStruct((nseq * t, MLA_WIDTH), F32),
        compiler_params=pltpu.CompilerParams(dimension_semantics=("arbitrary",),
                                             vmem_limit_bytes=VMEM_LIMIT),
        name="attn_sample",
    )(page_table.reshape(-1), q, c_new, kp_new, ga, w_kbp, w_kb, w_vb, kn_w, ones_top, cache_ckv,
      cache_kpet)


def _outproj_kernel(or_ref, oa_ref, x_ref, w_ref, y_ref):
    y = _dot(or_ref[...].astype(BF16), w_ref[:RET_WIDTH, :])
    y = y + _dot(oa_ref[...].astype(BF16), w_ref[RET_WIDTH:, :])
    y_ref[...] = x_ref[...] + y


def _outproj(o_r, o_a, x, w_out, *, tm):
    n = x.shape[0]
    row = lambda i: (i, 0)
    return pl.pallas_call(
        _outproj_kernel,
        grid=(n // tm,),
        in_specs=[pl.BlockSpec((tm, RET_WIDTH), row), pl.BlockSpec((tm, MLA_WIDTH), row),
                  pl.BlockSpec((tm, D_MODEL), row), pl.BlockSpec(w_out.shape, lambda i: (0, 0))],
        out_specs=pl.BlockSpec((tm, D_MODEL), row),
        out_shape=jax.ShapeDtypeStruct(x.shape, F32),
        compiler_params=pltpu.CompilerParams(dimension_semantics=("arbitrary",),
                                             vmem_limit_bytes=VMEM_LIMIT),
        name="outproj",
    )(o_r, o_a, x, w_out)


def _rope_tables(pos):
    def cs(dim):
        inv = ROPE_BASE ** (-(jnp.arange(0, dim, 2, dtype=F32) / dim))
        ang = pos.astype(F32)[:, None] * inv[None, :]
        return jnp.cos(ang), jnp.sin(ang)

    cr, sr = cs(RET_DK)
    ca, sa = cs(MLA_ROPE)
    zeros = jnp.zeros((pos.shape[0], LANES - MLA_ROPE), F32)
    return (jnp.concatenate([cr, cr], axis=1), jnp.concatenate([-sr, sr], axis=1),
            jnp.concatenate([ca, ca, zeros], axis=1), jnp.concatenate([-sa, sa, zeros], axis=1))


def _decay_tables(chunk, nseq):
    log_g = jnp.log1p(-jnp.exp2(-5.0 - jnp.arange(RET_HEADS, dtype=F32)))
    i = jnp.arange(chunk, dtype=F32)
    diff = i[:, None] - i[None, :]
    intra = jnp.where(diff >= 0, jnp.exp(log_g[:, None, None] * jnp.maximum(diff, 0.0)), 0.0)
    if nseq > 1:
        same = jnp.eye(nseq, dtype=F32)
        intra = jnp.einsum("ab,hij->haibj", same, intra).reshape(RET_HEADS, nseq * chunk,
                                                                 nseq * chunk)
    q_dec = jnp.exp(log_g[None, :] * (i[:, None] + 1.0))
    k_dec = jnp.exp(log_g[None, :] * (chunk - 1.0 - i[:, None]))
    widen = lambda a: jnp.tile(jnp.repeat(a, RET_DK, axis=1), (nseq, 1))
    g = 1.0 - np.exp2(-5.0 - np.arange(RET_HEADS, dtype=np.float64))
    sdec = tuple(float(v) for v in np.exp(np.log(g) * chunk))
    return {"intra": intra, "qdec": widen(q_dec), "kdec": widen(k_dec), "sdec": sdec}


def _prep_weights(norm_w, w_in, q_a_norm_w, w_qb, qn_w, qp_w, kv_a_norm_w, kp_w, w_kb, kn_w, w_vb,
                  ret_gn_w, w_out):
    depth = w_in.shape[0]
    s = np.cumsum((512, 512, 512, 512, Q_RANK, KV_RANK, MLA_ROPE, MLA_WIDTH))
    w_in_r = jnp.pad(w_in[:, :, :s[6]], ((0, 0), (0, 0), (0, LANES - MLA_ROPE))).astype(BF16)
    w_ga = w_in[:, :, s[6]:].astype(BF16)
    wq = w_qb.reshape(depth, Q_RANK, MLA_HEADS, MLA_NOPE + MLA_ROPE)
    wq = jnp.pad(wq, ((0, 0), (0, 0), (0, 0), (0, QHEAD - MLA_NOPE - MLA_ROPE)))
    wq = wq.reshape(depth, Q_RANK, QCOLS).astype(BF16)
    pad_rope = lambda w: jnp.pad(w, ((0, 0), (0, LANES - MLA_ROPE)))
    layers = []
    for l in range(depth):
        layers.append({
            "norm_w": norm_w[l][None], "w_in": w_in_r[l], "w_ga": w_ga[l], "q_a_norm_w": q_a_norm_w[l][None],
            "w_qb": wq[l], "qn_w": qn_w[l][None], "qp_w": pad_rope(qp_w)[l][None],
            "kv_a_norm_w": kv_a_norm_w[l][None], "kp_w": pad_rope(kp_w)[l][None],
            "w_kb": w_kb[l].astype(BF16), "kn_w": kn_w[l][None], "w_vb": w_vb[l].astype(BF16),
            "ret_gn_w": ret_gn_w[l][None], "w_out": w_out[l].astype(BF16)})
    return layers


def kernel(x_prompt, x_sample, cache_ckv, cache_kpe, state_ret, page_table, norm_w, w_in, q_a_norm_w,
           w_qb, qn_w, qp_w, kv_a_norm_w, kp_w, w_kb, kn_w, w_vb, ret_gn_w, w_out):
    b_p, t_p, _ = x_prompt.shape
    b_s, t_s, _ = x_sample.shape
    depth = w_in.shape[0]
    past = page_table.shape[1] * PAGE_SIZE
    tm_p = 512
    tm_s = 512
    ret_nseq = 8
    ret_bb = 4
    ppb = 32
    cache_kpet = jnp.swapaxes(cache_kpe, 2, 3)

    layers = _prep_weights(norm_w, w_in, q_a_norm_w, w_qb, qn_w, qp_w, kv_a_norm_w, kp_w, w_kb, kn_w,
                           w_vb, ret_gn_w, w_out)
    tabs_p = _rope_tables(jnp.arange(t_p, dtype=jnp.int32))
    tabs_s = _rope_tables(past + jnp.arange(t_s, dtype=jnp.int32))
    tabs_s = tuple(jnp.tile(a, (tm_s // t_s, 1)) for a in tabs_s)
    dec_p = _decay_tables(RET_CHUNK, 1)
    dec_s = _decay_tables(t_s, ret_nseq)

    hp = x_prompt.reshape(b_p * t_p, D_MODEL)
    hs = x_sample.reshape(b_s * t_s, D_MODEL)
    outs = {k: [] for k in ("ckv_p", "kpe_p", "ret_p", "ckv_s", "kpe_s", "ret_s")}
    for l in range(depth):
        lw = layers[l]
        qr, kr, vr, gr, ga, q, c, kpe, kcat, v = _front(hp, tabs_p, lw, tm=tm_p, emit_kv=True,
                                                        act_dtype=BF16)
        o_r, s_new = _ret_prompt(qr, kr, vr, gr, dec_p, lw["ret_gn_w"], batch=b_p, seq=t_p,
                                 bb=ret_bb)
        o_a = _attn_prompt(q, kcat, v, ga, batch=b_p, seq=t_p, tq=512, tk=512)
        hp = _outproj(o_r, o_a, hp, lw["w_out"], tm=2 * tm_p)
        outs["ckv_p"].append(c.reshape(b_p, t_p, KV_RANK))
        outs["kpe_p"].append(kpe.reshape(b_p, t_p, MLA_ROPE))
        outs["ret_p"].append(s_new)

        qr, kr, vr, gr, ga, q, c, kpe, kp128 = _front(hs, tabs_s, lw, tm=tm_s, emit_kv=False,
                                                      act_dtype=F32)
        o_r, s_new = _ret_sample(qr, kr, vr, gr, state_ret, dec_s, lw["ret_gn_w"], layer=l, t=t_s,
                                 nseq=ret_nseq)
        o_a = _attn_sample(page_table, q, c, kp128, ga, lw["w_kb"], lw["w_vb"], lw["kn_w"],
                           cache_ckv, cache_kpet, layer=l, t=t_s, ppb=ppb)
        hs = _outproj(o_r, o_a, hs, lw["w_out"], tm=tm_s)
        outs["ckv_s"].append(c.reshape(b_s, t_s, KV_RANK))
        outs["kpe_s"].append(kpe.reshape(b_s, t_s, MLA_ROPE))
        outs["ret_s"].append(s_new)

    return (hp.reshape(b_p, t_p, D_MODEL), hs.reshape(b_s, t_s, D_MODEL),
            jnp.stack(outs["ckv_p"]), jnp.stack(outs["kpe_p"]), jnp.stack(outs["ret_p"]),
            jnp.stack(outs["ckv_s"]), jnp.stack(outs["kpe_s"]), jnp.stack(outs["ret_s"]))
```
